```python
import math
import jax, jax.numpy as jnp
from jax import lax
import numpy as np

D_MODEL = 2048
BATCH = 8
SEQ = 2048
DEPTH = 4
DEC_BATCH = 4
DEC_SEQ = 4096
PAST_LEN = 128

HEAD_DIM = 64
GRID_W = 64
NA_HEADS = 8
NA_ROWS = 8
NA_COLS = 16
MLA_HEADS = 8
MLA_Q_RANK = 512
MLA_KV_RANK = 256
MLA_NOPE = 64
MLA_ROPE = 32
MLA_V = 64
ROPE_THETA = 10000.0
SW_HEADS = 8
SW_KV_HEADS = 2
SW_RADIUS = 128
DIL_PAIRS = ((128, 1), (512, 4), (2048, 16))
DIL_GROUPS = 3
DIL_HEADS = 8
DIL_SIDE = 64
DIL_QBLOCK = 64
BLOCK = 128
T5_BUCKETS = 32
T5_MAX_DIST = 1024
T5_HEADS = SW_HEADS + DIL_GROUPS * DIL_HEADS
N_EXPERTS = 8
TOP_K = 2
D_FF = 5632
N_DENSE = (DEPTH + 1) // 2
N_MOE = DEPTH // 2
NORM_EPS = 1e-6
NA_IN_WIDTH = 3 * NA_HEADS * HEAD_DIM
MLA_IN_WIDTH = MLA_Q_RANK + MLA_KV_RANK + MLA_ROPE
SW_IN_WIDTH = (SW_HEADS + 2 * SW_KV_HEADS) * HEAD_DIM
DIL_IN_WIDTH = 3 * DIL_GROUPS * DIL_HEADS * HEAD_DIM
IN_WIDTH = NA_IN_WIDTH + MLA_IN_WIDTH + SW_IN_WIDTH + DIL_IN_WIDTH
MIX_WIDTH = (NA_HEADS + MLA_HEADS + SW_HEADS + DIL_HEADS) * HEAD_DIM

kernel_name = 'hymba_style_hybrid_bidir_encoder'


def rms_norm(x, g):
    xf = x.astype(jnp.float32)
    y = xf * lax.rsqrt(jnp.mean(xf * xf, axis=-1, keepdims=True) + NORM_EPS)
    return (y * g.astype(jnp.float32)).astype(x.dtype)


def modulate(h, shift, scale):
    return h * (1 + scale[:, None, :]) + shift[:, None, :]


def t5_bucket(rel):
    half = T5_BUCKETS // 2
    max_exact = half // 2
    side = jnp.where(rel > 0, half, 0)
    n = jnp.abs(rel)
    nf = jnp.maximum(n, 1).astype(jnp.float32)
    large = max_exact + (jnp.log(nf / max_exact) / math.log(T5_MAX_DIST / max_exact) * (half - max_exact)).astype(jnp.int32)
    large = jnp.minimum(large, half - 1)
    return side + jnp.where(n < max_exact, n, large)


def rope_tables(S, dim):
    inv = jnp.power(jnp.float32(ROPE_THETA), -jnp.arange(0, dim, 2, dtype=jnp.float32) / dim)
    ang = jnp.arange(S, dtype=jnp.float32)[:, None] * inv[None, :]
    return jnp.cos(ang), jnp.sin(ang)


def apply_rope(x, cos, sin):
    shp = cos.shape[:1] + (1,) * (x.ndim - 3) + cos.shape[1:]
    c = cos.reshape(shp).astype(x.dtype)
    s = sin.reshape(shp).astype(x.dtype)
    x1, x2 = jnp.split(x, 2, axis=-1)
    return jnp.concatenate([x1 * c - x2 * s, x1 * s + x2 * c], axis=-1)


def neighbourhood_attention(q, k, v, rpb):
    B, S, H, Dh = q.shape
    rows = S // GRID_W
    kr = min(NA_ROWS, rows)
    qg = q.reshape(B, rows, GRID_W, H, Dh)
    kg = k.reshape(B, rows, GRID_W, H, Dh)
    vg = v.reshape(B, rows, GRID_W, H, Dh)
    cols = jnp.arange(GRID_W)
    col_start = jnp.clip(cols - NA_COLS // 2, 0, GRID_W - NA_COLS)
    col_idx = col_start[:, None] + jnp.arange(NA_COLS)[None, :]
    col_bias_idx = col_idx - cols[:, None] + (NA_COLS - 1)
    scale = HEAD_DIM ** -0.5

    def one_row(r):
        r_start = jnp.clip(r - kr // 2, 0, rows - kr)
        row_idx = r_start + jnp.arange(kr)
        row_bias_idx = row_idx - r + (NA_ROWS - 1)
        q_r = lax.dynamic_index_in_dim(qg, r, axis=1, keepdims=False)
        k_win = jnp.take(kg, row_idx, axis=1)[:, :, col_idx]
        v_win = jnp.take(vg, row_idx, axis=1)[:, :, col_idx]
        s = jnp.einsum('bqhd,brqchd->bhqrc', q_r, k_win).astype(jnp.float32) * scale
        bias = rpb[:, row_bias_idx[None, :, None], col_bias_idx[:, None, :]]
        s = s + bias[None].astype(jnp.float32)
        p = jax.nn.softmax(s.reshape(B, H, GRID_W, kr * NA_COLS), axis=-1)
        p = p.reshape(B, H, GRID_W, kr, NA_COLS).astype(v.dtype)
        return jnp.einsum('bhqrc,brqchd->bqhd', p, v_win)

    out = lax.map(one_row, jnp.arange(rows))
    return out.transpose(1, 0, 2, 3, 4).reshape(B, S, H * Dh)


def latent_attention(c_q, c_kv, k_rope, g_q, g_kv, w_uq, w_ukv):
    B, S, _ = c_q.shape
    H = MLA_HEADS
    q = (rms_norm(c_q, g_q) @ w_uq).reshape(B, S, H, MLA_NOPE + MLA_ROPE)
    q_nope, q_pe = jnp.split(q, [MLA_NOPE], axis=-1)
    kv = (rms_norm(c_kv, g_kv) @ w_ukv).reshape(B, S, H, MLA_NOPE + MLA_V)
    k_nope, v = jnp.split(kv, [MLA_NOPE], axis=-1)
    cos, sin = rope_tables(S, MLA_ROPE)
    q_pe = apply_rope(q_pe, cos, sin)
    k_pe = apply_rope(k_rope, cos, sin)
    scale = (MLA_NOPE + MLA_ROPE) ** -0.5
    nb = S // BLOCK
    qn_b = q_nope.reshape(B, nb, BLOCK, H, MLA_NOPE).swapaxes(0, 1)
    qp_b = q_pe.reshape(B, nb, BLOCK, H, MLA_ROPE).swapaxes(0, 1)

    def one_block(args):
        qn, qp = args
        s = jnp.einsum('bqhd,bkhd->bhqk', qn, k_nope) + jnp.einsum('bqhr,bkr->bhqk', qp, k_pe)
        p = jax.nn.softmax(s.astype(jnp.float32) * scale, axis=-1).astype(v.dtype)
        return jnp.einsum('bhqk,bkhd->bqhd', p, v)

    o = lax.map(one_block, (qn_b, qp_b))
    return o.swapaxes(0, 1).reshape(B, S, H * MLA_V)


def windowed_gqa_sink(q, k, v, sinks, t5_table):
    B, S, Hq, Dh = q.shape
    Hkv = k.shape[2]
    G = Hq // Hkv
    nb = S // BLOCK
    pad = ((0, 0), (BLOCK, BLOCK), (0, 0), (0, 0))
    kb = jnp.pad(k, pad).reshape(B, nb + 2, BLOCK, Hkv, Dh)
    vb = jnp.pad(v, pad).reshape(B, nb + 2, BLOCK, Hkv, Dh)
    k_win = jnp.concatenate([kb[:, :-2], kb[:, 1:-1], kb[:, 2:]], axis=2)
    v_win = jnp.concatenate([vb[:, :-2], vb[:, 1:-1], vb[:, 2:]], axis=2)
    qb = q.reshape(B, nb, BLOCK, Hkv, G, Dh)
    qa = jnp.arange(BLOCK)
    kc = jnp.arange(3 * BLOCK)
    rel = kc[None, :] - BLOCK - qa[:, None]
    key_pos = (jnp.arange(nb)[:, None] - 1) * BLOCK + kc[None, :]
    valid = (jnp.abs(rel) <= SW_RADIUS)[None] & ((key_pos >= 0) & (key_pos < S))[:, None, :]
    bias = t5_table[t5_bucket(rel)][..., :SW_HEADS]
    bias = bias.transpose(2, 0, 1).reshape(Hkv, G, BLOCK, 3 * BLOCK).astype(jnp.float32)
    scale = Dh ** -0.5
    s = jnp.einsum('bnqkgd,bnckd->bnkgqc', qb, k_win).astype(jnp.float32) * scale + bias[None, None]
    s = jnp.where(valid[None, :, None, None], s, -jnp.inf)
    sk = sinks.astype(jnp.float32).reshape(Hkv, G)[None, None, :, :, None, None]
    m = jnp.maximum(jnp.max(s, axis=-1, keepdims=True), sk)
    e = jnp.exp(s - m)
    p = e / (jnp.sum(e, axis=-1, keepdims=True) + jnp.exp(sk - m))
    o = jnp.einsum('bnkgqc,bnckd->bnqkgd', p.astype(v.dtype), v_win)
    return o.reshape(B, S, Hq * Dh)


def dilated_mixture_attention(q, k, v, t5_table):
    B, S, NG, H, Dh = q.shape
    nb = S // DIL_QBLOCK
    dil = jnp.array([d for _, d in DIL_PAIRS], dtype=jnp.int32)
    offs = dil[:, None] * jnp.arange(-DIL_SIDE, DIL_SIDE + 1, dtype=jnp.int32)[None, :]
    tab = t5_table[:, SW_HEADS:].reshape(T5_BUCKETS, NG, H)
    bias = tab[t5_bucket(offs), jnp.arange(NG)[:, None]]
    bias = bias.transpose(0, 2, 1).astype(jnp.float32)
    kg = jnp.moveaxis(k, 2, 0)
    vg = jnp.moveaxis(v, 2, 0)
    qb = q.reshape(B, nb, DIL_QBLOCK, NG, H, Dh).swapaxes(0, 1)
    scale = Dh ** -0.5
    gather = jax.vmap(lambda kk, ii: kk[:, ii])

    def one_block(args):
        n, qblk = args
        pos = n * DIL_QBLOCK + jnp.arange(DIL_QBLOCK)
        idx = pos[None, :, None] + offs[:, None, :]
        valid = (idx >= 0) & (idx < S)
        idx = jnp.clip(idx, 0, S - 1)
        kw = gather(kg, idx)
        vw = gather(vg, idx)
        s = jnp.einsum('bqghd,gbqkhd->bghqk', qblk, kw).astype(jnp.float32) * scale + bias[None, :, :, None, :]
        s = jnp.where(valid[None, :, None], s, -jnp.inf)
        mx = jnp.max(s, axis=-1, keepdims=True)
        e = jnp.exp(s - mx)
        den = jnp.sum(e, axis=-1, keepdims=True)
        o_g = jnp.einsum('bghqk,gbqkhd->bghqd', (e / den).astype(v.dtype), vw)
        w = jax.nn.softmax(mx + jnp.log(den), axis=1)
        o = jnp.sum(w.astype(o_g.dtype) * o_g, axis=1)
        return o.transpose(0, 2, 1, 3)

    out = lax.map(one_block, (jnp.arange(nb), qb))
    return out.swapaxes(0, 1).reshape(B, S, H * Dh)


def token_mix(h, w_in, g_q, g_kv, w_uq, w_ukv, rpb, sinks, t5_table):
    B, S, _ = h.shape
    proj = h @ w_in
    a, b, c, d = jnp.split(proj, [NA_IN_WIDTH, NA_IN_WIDTH + MLA_IN_WIDTH, NA_IN_WIDTH + MLA_IN_WIDTH + SW_IN_WIDTH], axis=-1)
    qa, ka, va = [t.reshape(B, S, NA_HEADS, HEAD_DIM) for t in jnp.split(a, 3, axis=-1)]
    o_a = neighbourhood_attention(qa, ka, va, rpb)
    c_q, c_kv, k_rope = jnp.split(b, [MLA_Q_RANK, MLA_Q_RANK + MLA_KV_RANK], axis=-1)
    o_b = latent_attention(c_q, c_kv, k_rope, g_q, g_kv, w_uq, w_ukv)
    qs, ks, vs = jnp.split(c, [SW_HEADS * HEAD_DIM, (SW_HEADS + SW_KV_HEADS) * HEAD_DIM], axis=-1)
    o_c = windowed_gqa_sink(qs.reshape(B, S, SW_HEADS, HEAD_DIM), ks.reshape(B, S, SW_KV_HEADS, HEAD_DIM), vs.reshape(B, S, SW_KV_HEADS, HEAD_DIM), sinks, t5_table)
    dqkv = d.reshape(B, S, 3, DIL_GROUPS, DIL_HEADS, HEAD_DIM)
    o_d = dilated_mixture_attention(dqkv[:, :, 0], dqkv[:, :, 1], dqkv[:, :, 2], t5_table)
    return jnp.concatenate([o_a, o_b, o_c, o_d], axis=-1)


def swiglu(h, wg, wu, wd):
    return (jax.nn.silu(h @ wg) * (h @ wu)) @ wd


def moe_ffn(h, w_router, wg, wu, wd):
    B, S, D = h.shape
    t = h.reshape(B * S, D)
    logits = (t @ w_router).astype(jnp.float32)
    top_v, top_i = lax.top_k(logits, TOP_K)
    gates = jax.nn.softmax(top_v, axis=-1)
    comb = jnp.sum(jax.nn.one_hot(top_i, N_EXPERTS, dtype=jnp.float32) * gates[..., None], axis=1).astype(t.dtype)
    y = jnp.zeros_like(t)
    for e in range(N_EXPERTS):
        y = y + comb[:, e:e + 1] * swiglu(t, wg[e], wu[e], wd[e])
    return y.reshape(B, S, D)


def run_trunk(x, c, w_ada, b_ada, g_mix, g_ffn, w_in, mla_g_q, mla_g_kv, mla_w_uq, mla_w_ukv, na_rpb, sw_sinks, t5_table, w_out, ffn_w_gate, ffn_w_up, ffn_w_down, moe_w_router, moe_w_gate, moe_w_up, moe_w_down, g_final):
    cs = jax.nn.silu(c)
    for l in range(DEPTH):
        mod = cs @ w_ada[l] + b_ada[l]
        sh1, sc1, gt1, sh2, sc2, gt2 = jnp.split(mod, 6, axis=-1)
        h = modulate(rms_norm(x, g_mix[l]), sh1, sc1)
        o = token_mix(h, w_in[l], mla_g_q[l], mla_g_kv[l], mla_w_uq[l], mla_w_ukv[l], na_rpb[l], sw_sinks[l], t5_table)
        x = x + gt1[:, None, :] * (o @ w_out[l])
        h = modulate(rms_norm(x, g_ffn[l]), sh2, sc2)
        if l % 2 == 0:
            f = swiglu(h, ffn_w_gate[l // 2], ffn_w_up[l // 2], ffn_w_down[l // 2])
        else:
            f = moe_ffn(h, moe_w_router[l // 2], moe_w_gate[l // 2], moe_w_up[l // 2], moe_w_down[l // 2])
        x = x + gt2[:, None, :] * f
    return rms_norm(x, g_final)


def setup_inputs(seed: int = 0) -> dict:
    key = jax.random.key(seed)
    ks = jax.random.split(key, 32)
    f32 = jnp.float32
    nrm = lambda k, shape, s: jax.random.normal(k, shape, f32) * s
    D = D_MODEL
    return {
        'x_prompt': nrm(ks[0], (BATCH, SEQ, D), 1.0),
        'x_sample': nrm(ks[1], (DEC_BATCH, DEC_SEQ, D), 1.0),
        'c_prompt': nrm(ks[2], (BATCH, D), 1.0),
        'c_sample': nrm(ks[3], (DEC_BATCH, D), 1.0),
        'w_ada': nrm(ks[4], (DEPTH, D, 6 * D), 0.5 * D ** -0.5),
        'b_ada': nrm(ks[5], (DEPTH, 6 * D), 0.02),
        'g_mix': 1.0 + nrm(ks[6], (DEPTH, D), 0.05),
        'g_ffn': 1.0 + nrm(ks[7], (DEPTH, D), 0.05),
        'w_in': nrm(ks[8], (DEPTH, D, IN_WIDTH), D ** -0.5),
        'mla_g_q': 1.0 + nrm(ks[9], (DEPTH, MLA_Q_RANK), 0.05),
        'mla_g_kv': 1.0 + nrm(ks[10], (DEPTH, MLA_KV_RANK), 0.05),
        'mla_w_uq': nrm(ks[11], (DEPTH, MLA_Q_RANK, MLA_HEADS * (MLA_NOPE + MLA_ROPE)), MLA_Q_RANK ** -0.5),
        'mla_w_ukv': nrm(ks[12], (DEPTH, MLA_KV_RANK, MLA_HEADS * (MLA_NOPE + MLA_V)), MLA_KV_RANK ** -0.5),
        'na_rpb': nrm(ks[13], (DEPTH, NA_HEADS, 2 * NA_ROWS - 1, 2 * NA_COLS - 1), 0.2),
        'sw_sinks': nrm(ks[14], (DEPTH, SW_HEADS), 1.0),
        't5_table': nrm(ks[15], (T5_BUCKETS, T5_HEADS), 0.2),
        'w_out': nrm(ks[16], (DEPTH, MIX_WIDTH, D), MIX_WIDTH ** -0.5),
        'ffn_w_gate': nrm(ks[17], (N_DENSE, D, D_FF), D ** -0.5),
        'ffn_w_up': nrm(ks[18], (N_DENSE, D, D_FF), D ** -0.5),
        'ffn_w_down': nrm(ks[19], (N_DENSE, D_FF, D), D_FF ** -0.5),
        'moe_w_router': nrm(ks[20], (N_MOE, D, N_EXPERTS), D ** -0.5),
        'moe_w_gate': nrm(ks[21], (N_MOE, N_EXPERTS, D, D_FF), D ** -0.5),
        'moe_w_up': nrm(ks[22], (N_MOE, N_EXPERTS, D, D_FF), D ** -0.5),
        'moe_w_down': nrm(ks[23], (N_MOE, N_EXPERTS, D_FF, D), D_FF ** -0.5),
        'g_final': 1.0 + nrm(ks[24], (D,), 0.05),
    }


def reference(x_prompt, x_sample, c_prompt, c_sample, w_ada, b_ada, g_mix, g_ffn, w_in, mla_g_q, mla_g_kv, mla_w_uq, mla_w_ukv, na_rpb, sw_sinks, t5_table, w_out, ffn_w_gate, ffn_w_up, ffn_w_down, moe_w_router, moe_w_gate, moe_w_up, moe_w_down, g_final):
    y_prompt = run_trunk(x_prompt, c_prompt, w_ada, b_ada, g_mix, g_ffn, w_in, mla_g_q, mla_g_kv, mla_w_uq, mla_w_ukv, na_rpb, sw_sinks, t5_table, w_out, ffn_w_gate, ffn_w_up, ffn_w_down, moe_w_router, moe_w_gate, moe_w_up, moe_w_down, g_final)
    y_sample = run_trunk(x_sample, c_sample, w_ada, b_ada, g_mix, g_ffn, w_in, mla_g_q, mla_g_kv, mla_w_uq, mla_w_ukv, na_rpb, sw_sinks, t5_table, w_out, ffn_w_gate, ffn_w_up, ffn_w_down, moe_w_router, moe_w_gate, moe_w_up, moe_w_down, g_final)
    return (y_prompt, y_sample)
```

```python
import functools
import math

import numpy as np
import jax
import jax.numpy as jnp
from jax import lax
from jax.experimental import pallas as pl
from jax.experimental.pallas import tpu as pltpu

F32 = jnp.float32
BF16 = jnp.bfloat16

D_MODEL = 2048
DEPTH = 4
HEAD_DIM = 64
GRID_W = 64
NA_HEADS = 8
NA_ROWS = 8
NA_COLS = 16
MLA_HEADS = 8
MLA_Q_RANK = 512
MLA_KV_RANK = 256
MLA_NOPE = 64
MLA_ROPE = 32
MLA_V = 64
ROPE_THETA = 10000.0
SW_HEADS = 8
SW_KV_HEADS = 2
SW_RADIUS = 128
DIL_DILATIONS = (1, 4, 16)
DIL_GROUPS = 3
DIL_HEADS = 8
DIL_SIDE = 64
T5_BUCKETS = 32
T5_MAX_DIST = 1024
N_EXPERTS = 8
D_FF = 5632
NORM_EPS = 1e-6

LANES = 128
NEG = -1e30
VMEM_LIMIT = 56 * 1024 * 1024

NA_IN = 3 * NA_HEADS * HEAD_DIM
MLA_IN = MLA_Q_RANK + MLA_KV_RANK + MLA_ROPE
SW_IN = (SW_HEADS + 2 * SW_KV_HEADS) * HEAD_DIM
DIL_IN = 3 * DIL_GROUPS * DIL_HEADS * HEAD_DIM
IN_WIDTH = NA_IN + MLA_IN + SW_IN + DIL_IN
P_NA = 0
P_CQ = 1536
P_CKV = 2048
P_KR = 2304
P_KRR = 2432
P_SWQ = 2560
P_SWK = 3072
P_SWV = 3200
P_DIL = 3584
P_WIDTH = 8192
SW_PERM = (0, 4, 1, 5, 2, 6, 3, 7)


def _cparams(*sem):
    return pltpu.CompilerParams(dimension_semantics=sem, vmem_limit_bytes=VMEM_LIMIT)


def _nt_dot(a, b):
    return lax.dot_general(a, b, (((1,), (1,)), ((), ())), preferred_element_type=F32)


def _dot(a, b):
    return jnp.dot(a, b, preferred_element_type=F32)


def _rot_half_cols(base):
    half = MLA_ROPE // 2
    src = [base + half + j for j in range(half)] + [base + j for j in range(half)]
    sgn = [-1.0] * half + [1.0] * half
    return src, sgn


def _in_proj_colmap():
    src = np.zeros((P_WIDTH,), np.int32)
    mul = np.zeros((P_WIDTH,), np.float32)
    qs = HEAD_DIM ** -0.5
    for j in range(NA_IN):
        src[P_NA + j] = j
        mul[P_NA + j] = qs if j < NA_HEADS * HEAD_DIM else 1.0
    b0 = NA_IN
    for j in range(MLA_Q_RANK + MLA_KV_RANK):
        src[P_CQ + j] = b0 + j
        mul[P_CQ + j] = 1.0
    kr0 = b0 + MLA_Q_RANK + MLA_KV_RANK
    rsrc, rsgn = _rot_half_cols(kr0)
    for rep in range(2):
        for j in range(MLA_ROPE):
            src[P_KR + rep * MLA_ROPE + j] = kr0 + j
            mul[P_KR + rep * MLA_ROPE + j] = 1.0
            src[P_KRR + rep * MLA_ROPE + j] = rsrc[j]
            mul[P_KRR + rep * MLA_ROPE + j] = rsgn[j]
    c0 = NA_IN + MLA_IN
    for hh, h in enumerate(SW_PERM):
        for d in range(HEAD_DIM):
            src[P_SWQ + hh * HEAD_DIM + d] = c0 + h * HEAD_DIM + d
            mul[P_SWQ + hh * HEAD_DIM + d] = qs
    for j in range(2 * SW_KV_HEADS * HEAD_DIM):
        src[P_SWK + j] = c0 + SW_HEADS * HEAD_DIM + j
        mul[P_SWK + j] = 1.0
    d0 = NA_IN + MLA_IN + SW_IN
    for j in range(DIL_IN):
        src[P_DIL + j] = d0 + j
        mul[P_DIL + j] = qs if j < DIL_GROUPS * DIL_HEADS * HEAD_DIM else 1.0
    return src, mul


def _mla_q_colmap():
    n = 3 * 4 * LANES
    src = np.zeros((n,), np.int32)
    mul = np.zeros((n,), np.float32)
    hw = MLA_NOPE + MLA_ROPE
    for h in range(MLA_HEADS):
        p, a = divmod(h, 2)
        for d in range(MLA_NOPE):
            src[p * LANES + a * MLA_NOPE + d] = h * hw + d
            mul[p * LANES + a * MLA_NOPE + d] = 1.0
        rsrc, rsgn = _rot_half_cols(h * hw + MLA_NOPE)
        for j in range(MLA_ROPE):
            ca = 4 * LANES + p * LANES + a * MLA_ROPE + j
            cb = 8 * LANES + p * LANES + a * MLA_ROPE + j
            src[ca] = h * hw + MLA_NOPE + j
            mul[ca] = 1.0
            src[cb] = rsrc[j]
            mul[cb] = rsgn[j]
    return src, mul


def _mla_kv_colmap():
    n = 2 * 4 * LANES
    src = np.zeros((n,), np.int32)
    hw = MLA_NOPE + MLA_V
    for h in range(MLA_HEADS):
        for d in range(MLA_NOPE):
            src[h * MLA_NOPE + d] = h * hw + d
            src[4 * LANES + h * MLA_V + d] = h * hw + MLA_NOPE + d
    return src


def _out_proj_rowmap():
    src = np.arange(4 * 512, dtype=np.int32)
    for hh, h in enumerate(SW_PERM):
        for d in range(HEAD_DIM):
            src[1024 + hh * HEAD_DIM + d] = 1024 + h * HEAD_DIM + d
    return src


_IN_SRC, _IN_MUL = _in_proj_colmap()
_UQ_SRC, _UQ_MUL = _mla_q_colmap()
_UKV_SRC = _mla_kv_colmap()
_OUT_SRC = _out_proj_rowmap()


def _t5_bucket(rel):
    half = T5_BUCKETS // 2
    max_exact = half // 2
    side = jnp.where(rel > 0, half, 0)
    n = jnp.abs(rel)
    nf = jnp.maximum(n, 1).astype(F32)
    large = max_exact + (jnp.log(nf / max_exact) / math.log(T5_MAX_DIST / max_exact) * (half - max_exact)).astype(jnp.int32)
    large = jnp.minimum(large, half - 1)
    return side + jnp.where(n < max_exact, n, large)


def _band_bias(table_hb, radius, dilation, tq, kw):
    qa = np.arange(tq)[:, None]
    kc = np.arange(kw)[None, :]
    outs = []
    for v in range(3):
        rel = kc - qa - v * radius
        valid = jnp.asarray(np.abs(rel) <= radius)
        b = table_hb[:, _t5_bucket(jnp.asarray(rel * dilation, jnp.int32))]
        outs.append(jnp.where(valid[None], b, NEG))
    return jnp.stack(outs, axis=1).astype(F32)


def _na_bias(rpb):
    c = np.arange(GRID_W)[:, None]
    kc = np.arange(GRID_W)[None, :]
    col_start = np.clip(c - NA_COLS // 2, 0, GRID_W - NA_COLS)
    valid = (kc >= col_start) & (kc < col_start + NA_COLS)
    cidx = np.clip(kc - c + NA_COLS - 1, 0, 2 * NA_COLS - 2)
    d = np.arange(NA_ROWS)[:, None]
    kr = np.arange(NA_ROWS)[None, :]
    ridx = kr - d + NA_ROWS - 1
    b = rpb[:, ridx[:, :, None, None], cidx[None, None, :, :]]
    b = jnp.where(jnp.asarray(valid)[None, None, None], b, NEG)
    b = b.transpose(0, 1, 3, 2, 4)
    return b.reshape(NA_HEADS, NA_ROWS, GRID_W, NA_ROWS * GRID_W).astype(F32)


def _rope_tables(S):
    inv = jnp.power(jnp.float32(ROPE_THETA), -jnp.arange(0, MLA_ROPE, 2, dtype=F32) / MLA_ROPE)
    ang = jnp.arange(S, dtype=F32)[:, None] * inv[None, :]
    cos, sin = jnp.cos(ang), jnp.sin(ang)
    z = jnp.zeros((S, LANES - 2 * MLA_ROPE), F32)
    cos_l = jnp.concatenate([cos, cos, cos, cos, z], axis=1)
    sin_l = jnp.concatenate([sin, sin, sin, sin, z], axis=1)
    return cos_l, sin_l


def _ada_kernel(c_ref, w_ref, b_ref, o_ref):
    c = c_ref[...]
    cs = (c * jax.nn.sigmoid(c)).astype(BF16)
    o_ref[...] = _dot(cs, w_ref[...].astype(BF16)) + b_ref[...]


def _ada(c_pad, w_ada, b_ada):
    L, D, N = w_ada.shape
    R = c_pad.shape[0]
    tn = 1024
    return pl.pallas_call(
        _ada_kernel,
        grid=(L, N // tn),
        in_specs=[
            pl.BlockSpec((R, D), lambda l, j: (0, 0)),
            pl.BlockSpec((None, D, tn), lambda l, j: (l, 0, j)),
            pl.BlockSpec((None, 1, tn), lambda l, j: (l, 0, j)),
        ],
        out_specs=pl.BlockSpec((None, R, tn), lambda l, j: (l, 0, j)),
        out_shape=jax.ShapeDtypeStruct((L, R, N), F32),
        compiler_params=_cparams("parallel", "parallel"),
        name="ada",
    )(c_pad, w_ada, b_ada.reshape(L, 1, N))


def _norm_mod(x, g, sh, sc):
    y = x * lax.rsqrt(jnp.mean(x * x, axis=-1, keepdims=True) + NORM_EPS) * g
    return y * (1.0 + sc) + sh


def _mod_spec(k, tm, S):
    return pl.BlockSpec((None, None, 1, D_MODEL), lambda i, *_: ((i * tm) // S, k, 0, 0))


def _in_kernel(x_ref, g_ref, sh_ref, sc_ref, w_ref, o_ref, hn_ref):
    @pl.when(pl.program_id(1) == 0)
    def _():
        hn_ref[...] = _norm_mod(x_ref[...], g_ref[...], sh_ref[...], sc_ref[...]).astype(BF16)

    o_ref[...] = _dot(hn_ref[...], w_ref[...]).astype(BF16)


def _in_proj(x, g, mod, w, S):
    T, D = x.shape
    N = w.shape[1]
    tm, tn = 1024, 1024
    return pl.pallas_call(
        _in_kernel,
        grid=(T // tm, N // tn),
        in_specs=[
            pl.BlockSpec((tm, D), lambda i, j: (i, 0)),
            pl.BlockSpec((1, D), lambda i, j: (0, 0)),
            _mod_spec(0, tm, S),
            _mod_spec(1, tm, S),
            pl.BlockSpec((D, tn), lambda i, j: (0, j)),
        ],
        out_specs=pl.BlockSpec((tm, tn), lambda i, j: (i, j)),
        out_shape=jax.ShapeDtypeStruct((T, N), BF16),
        scratch_shapes=[pltpu.VMEM((tm, D), BF16)],
        compiler_params=_cparams("parallel", "arbitrary"),
        name="in_proj",
    )(x, g.reshape(1, D), mod, mod, w)


def _lane_half_mask(shape):
    return lax.broadcasted_iota(jnp.int32, shape, len(shape) - 1) < HEAD_DIM


def _softmax_pv(s, v):
    m = jnp.max(s, axis=-1, keepdims=True)
    p = jnp.exp(s - m)
    l = jnp.sum(p, axis=-1, keepdims=True)
    return _dot(p.astype(BF16), v) / l


def _na_kernel(q_ref, k_ref, v_ref, bias_ref, o_ref, *, rows, qrows):
    rb = pl.program_id(1)
    kwin = NA_ROWS * GRID_W
    lo = _lane_half_mask((GRID_W, LANES))
    for qr in range(qrows):
        r = rb * qrows + qr
        r_start = jnp.clip(r - NA_ROWS // 2, 0, rows - NA_ROWS)
        d = r - r_start
        k0 = pl.multiple_of(r_start * GRID_W, GRID_W)
        for p in range(NA_HEADS // 2):
            cs = slice(p * LANES, (p + 1) * LANES)
            q = q_ref[qr * GRID_W:(qr + 1) * GRID_W, cs]
            k = k_ref[pl.ds(k0, kwin), cs]
            v = v_ref[pl.ds(k0, kwin), cs]
            zero = jnp.zeros_like(q)
            o0 = _softmax_pv(_nt_dot(jnp.where(lo, q, zero), k) + bias_ref[2 * p, d], v)
            o1 = _softmax_pv(_nt_dot(jnp.where(lo, zero, q), k) + bias_ref[2 * p + 1, d], v)
            o_ref[qr * GRID_W:(qr + 1) * GRID_W, cs] = jnp.where(lo, o0, o1).astype(BF16)


def _na_attn(proj, bias, B, S):
    T = B * S
    rows = S // GRID_W
    qrows = 8
    tq = qrows * GRID_W
    nqb = S // tq
    w = NA_HEADS * HEAD_DIM
    kern = functools.partial(_na_kernel, rows=rows, qrows=qrows)
    return pl.pallas_call(
        kern,
        grid=(B, nqb),
        in_specs=[
            pl.BlockSpec((tq, w), lambda b, r: (b * nqb + r, P_NA // w)),
            pl.BlockSpec((S, w), lambda b, r: (b, P_NA // w + 1)),
            pl.BlockSpec((S, w), lambda b, r: (b, P_NA // w + 2)),
            pl.BlockSpec(bias.shape, lambda b, r: (0, 0, 0, 0)),
        ],
        out_specs=pl.BlockSpec((tq, w), lambda b, r: (b * nqb + r, 0)),
        out_shape=jax.ShapeDtypeStruct((T, w), BF16),
        compiler_params=_cparams("parallel", "arbitrary"),
        name="na_attn",
    )(proj, proj, proj, bias)


def _mla_up_kernel(cq_ref, ckv_ref, kr_ref, krr_ref, cos_ref, sin_ref, gq_ref, gkv_ref, wq_ref, wkv_ref,
                   q_ref, k_ref, v_ref):
    scale = (MLA_NOPE + MLA_ROPE) ** -0.5
    cos = cos_ref[...]
    sin = sin_ref[...]

    def rms(x_ref, g_ref):
        x = x_ref[...].astype(F32)
        return (x * lax.rsqrt(jnp.mean(x * x, axis=-1, keepdims=True) + NORM_EPS) * g_ref[...]).astype(BF16)

    qf = _dot(rms(cq_ref, gq_ref), wq_ref[...]) * scale
    kvf = _dot(rms(ckv_ref, gkv_ref), wkv_ref[...])
    kpe = (kr_ref[...].astype(F32) * cos + krr_ref[...].astype(F32) * sin).astype(BF16)
    for p in range(MLA_HEADS // 2):
        nope = qf[:, p * LANES:(p + 1) * LANES]
        pe = qf[:, (4 + p) * LANES:(5 + p) * LANES] * cos + qf[:, (8 + p) * LANES:(9 + p) * LANES] * sin
        q_ref[:, 2 * p * LANES:(2 * p + 1) * LANES] = nope.astype(BF16)
        q_ref[:, (2 * p + 1) * LANES:(2 * p + 2) * LANES] = pe.astype(BF16)
        k_ref[:, 2 * p * LANES:(2 * p + 1) * LANES] = kvf[:, p * LANES:(p + 1) * LANES].astype(BF16)
        k_ref[:, (2 * p + 1) * LANES:(2 * p + 2) * LANES] = kpe
    v_ref[...] = kvf[:, 4 * LANES:].astype(BF16)


def _mla_up(proj, cos_l, sin_l, g_q, g_kv, wq, wkv, S):
    T = proj.shape[0]
    tm = 512
    nsb = S // tm
    row = lambda i: (i, 0)
    return pl.pallas_call(
        _mla_up_kernel,
        grid=(T // tm,),
        in_specs=[
            pl.BlockSpec((tm, MLA_Q_RANK), lambda i: (i, P_CQ // MLA_Q_RANK)),
            pl.BlockSpec((tm, MLA_KV_RANK), lambda i: (i, P_CKV // MLA_KV_RANK)),
            pl.BlockSpec((tm, LANES), lambda i: (i, P_KR // LANES)),
            pl.BlockSpec((tm, LANES), lambda i: (i, P_KRR // LANES)),
            pl.BlockSpec((tm, LANES), lambda i: (i % nsb, 0)),
            pl.BlockSpec((tm, LANES), lambda i: (i % nsb, 0)),
            pl.BlockSpec((1, MLA_Q_RANK), lambda i: (0, 0)),
            pl.BlockSpec((1, MLA_KV_RANK), lambda i: (0, 0)),
            pl.BlockSpec(wq.shape, lambda i: (0, 0)),
            pl.BlockSpec(wkv.shape, lambda i: (0, 0)),
        ],
        out_specs=[
            pl.BlockSpec((tm, 8 * LANES), row),
            pl.BlockSpec((tm, 8 * LANES), row),
            pl.BlockSpec((tm, 4 * LANES), row),
        ],
        out_shape=[
            jax.ShapeDtypeStruct((T, 8 * LANES), BF16),
            jax.ShapeDtypeStruct((T, 8 * LANES), BF16),
            jax.ShapeDtypeStruct((T, 4 * LANES), BF16),
        ],
        compiler_params=_cparams("parallel"),
        name="mla_up",
    )(proj, proj, proj, proj, cos_l, sin_l, g_q.reshape(1, -1), g_kv.reshape(1, -1), wq, wkv)


def _mla_attn_kernel(q_ref, k_ref, v_ref, o_ref, *, S, tk):
    tq = q_ref.shape[0]
    q = q_ref[...]
    lane = lax.broadcasted_iota(jnp.int32, q.shape, 1)
    first = ((lane < MLA_NOPE) | ((lane >= LANES) & (lane < LANES + MLA_ROPE)))
    zero = jnp.zeros_like(q)
    q0 = jnp.where(first, q, zero)
    q1 = jnp.where(first, zero, q)

    def body(c, carry):
        m0, l0, a0, m1, l1, a1 = carry
        k0 = pl.multiple_of(c * tk, tk)
        k = k_ref[pl.ds(k0, tk), :]
        v = v_ref[pl.ds(k0, tk), :]

        def step(qh, m, l, a):
            s = _nt_dot(qh, k)
            mn = jnp.maximum(m, jnp.max(s, axis=-1, keepdims=True))
            alpha = jnp.exp(m - mn)
            p = jnp.exp(s - mn)
            return mn, alpha * l + jnp.sum(p, axis=-1, keepdims=True), alpha * a + _dot(p.astype(BF16), v)

        m0, l0, a0 = step(q0, m0, l0, a0)
        m1, l1, a1 = step(q1, m1, l1, a1)
        return m0, l0, a0, m1, l1, a1

    mi = jnp.full((tq, 1), NEG, F32)
    li = jnp.zeros((tq, 1), F32)
    ai = jnp.zeros((tq, LANES), F32)
    m0, l0, a0, m1, l1, a1 = lax.fori_loop(0, S // tk, body, (mi, li, ai, mi, li, ai))
    lo = _lane_half_mask((tq, LANES))
    o_ref[...] = jnp.where(lo, a0 / l0, a1 / l1).astype(BF16)


def _mla_attn(qm, km, vm, B, S):
    T = B * S
    tq, tk = 256, 512
    nqb = S // tq
    kern = functools.partial(_mla_attn_kernel, S=S, tk=tk)
    return pl.pallas_call(
        kern,
        grid=(B, MLA_HEADS // 2, nqb),
        in_specs=[
            pl.BlockSpec((tq, 2 * LANES), lambda b, p, i: (b * nqb + i, p)),
            pl.BlockSpec((S, 2 * LANES), lambda b, p, i: (b, p)),
            pl.BlockSpec((S, LANES), lambda b, p, i: (b, p)),
        ],
        out_specs=pl.BlockSpec((tq, LANES), lambda b, p, i: (b * nqb + i, p)),
        out_shape=jax.ShapeDtypeStruct((T, MLA_HEADS * MLA_V), BF16),
        compiler_params=_cparams("parallel", "parallel", "arbitrary"),
        name="mla_attn",
    )(qm, km, vm)


def _band_kernel(*refs, L, radius, tq, kw, n_pairs, shared_kv, with_sink, with_lse):
    if with_sink:
        sink_ref, refs = refs[0], refs[1:]
    q_ref, k_ref, v_ref, bias_ref = refs[:4]
    o_ref = refs[4]
    lse_ref = refs[5] if with_lse else None
    qb = pl.program_id(2)
    nq = q_ref.shape[0] // tq
    lo = _lane_half_mask((tq, LANES))

    def q_tile(t, carry):
        q0 = (qb * nq + t) * tq
        start = jnp.clip(q0 - radius, 0, L - kw)
        var = (q0 - start) // radius
        k0 = pl.multiple_of(start, min(radius, tq))
        r0 = pl.multiple_of(t * tq, tq)
        for p in range(n_pairs):
            cs = slice(p * LANES, (p + 1) * LANES)
            kcs = slice(0, LANES) if shared_kv else cs
            q = q_ref[pl.ds(r0, tq), cs]
            k = k_ref[pl.ds(k0, kw), kcs]
            v = v_ref[pl.ds(k0, kw), kcs]
            zero = jnp.zeros_like(q)
            outs = []
            lses = []
            for a in range(2):
                qh = jnp.where(lo, q, zero) if a == 0 else jnp.where(lo, zero, q)
                s = _nt_dot(qh, k) + bias_ref[2 * p + a, var]
                m = jnp.max(s, axis=-1, keepdims=True)
                if with_sink:
                    sk = sink_ref[2 * p + a]
                    m = jnp.maximum(m, sk)
                e = jnp.exp(s - m)
                l = jnp.sum(e, axis=-1, keepdims=True)
                if with_sink:
                    l = l + jnp.exp(sk - m)
                outs.append(_dot(e.astype(BF16), v) / l)
                lses.append(m + jnp.log(l))
            if with_lse:
                o_ref[pl.ds(r0, tq), cs] = jnp.where(lo, outs[0], outs[1])
                lse_ref[pl.ds(r0, tq), cs] = jnp.where(lo, lses[0], lses[1])
            else:
                o_ref[pl.ds(r0, tq), cs] = jnp.where(lo, outs[0], outs[1]).astype(BF16)
        return carry

    lax.fori_loop(0, nq, q_tile, 0)


def _band_attn(proj, bias, *, B, S, dil, q_col, k_col, v_col, radius, shared_kv, sinks=None, with_lse=False):
    T, W = proj.shape
    L = S // dil
    tq = LANES
    kw = min(L, tq + 2 * radius)
    qblk = min(L, 512)
    nqb = L // qblk
    wq = SW_HEADS * HEAD_DIM
    wkv = LANES if shared_kv else wq
    view = proj.reshape(T // dil, dil * W)
    n_pairs = wq // LANES
    kern = functools.partial(_band_kernel, L=L, radius=radius, tq=tq, kw=kw, n_pairs=n_pairs,
                             shared_kv=shared_kv, with_sink=sinks is not None, with_lse=with_lse)
    in_specs = [
        pl.BlockSpec((qblk, wq), lambda b, c, i: (b * nqb + i, (c * W + q_col) // wq)),
        pl.BlockSpec((L, wkv), lambda b, c, i: (b, (c * W + k_col) // wkv)),
        pl.BlockSpec((L, wkv), lambda b, c, i: (b, (c * W + v_col) // wkv)),
        pl.BlockSpec(bias.shape, lambda b, c, i: (0, 0, 0, 0)),
    ]
    args = [view, view, view, bias]
    if sinks is not None:
        in_specs = [pl.BlockSpec(memory_space=pltpu.SMEM)] + in_specs
        args = [sinks] + args
    out_dtype = F32 if with_lse else BF16
    o_spec = pl.BlockSpec((qblk, wq), lambda b, c, i: (b * nqb + i, c))
    o_shape = jax.ShapeDtypeStruct((T // dil, dil * wq), out_dtype)
    out = pl.pallas_call(
        kern,
        grid=(B, dil, nqb),
        in_specs=in_specs,
        out_specs=[o_spec, o_spec] if with_lse else o_spec,
        out_shape=[o_shape, o_shape] if with_lse else o_shape,
        compiler_params=_cparams("parallel", "parallel", "arbitrary"),
        name="band_attn_d%d" % dil if with_lse else "band_attn_sw",
    )(*args)
    if with_lse:
        return out[0].reshape(T, wq), out[1].reshape(T, wq)
    return out.reshape(T, wq)


def _merge_kernel(o0, o1, o2, l0, l1, l2, o_ref):
    a, b, c = l0[...], l1[...], l2[...]
    m = jnp.maximum(jnp.maximum(a, b), c)
    wa, wb, wc = jnp.exp(a - m), jnp.exp(b - m), jnp.exp(c - m)
    o_ref[...] = ((wa * o0[...] + wb * o1[...] + wc * o2[...]) / (wa + wb + wc)).astype(BF16)


def _merge(os_, ls_):
    T, w = os_[0].shape
    tm = 1024
    spec = pl.BlockSpec((tm, w), lambda i: (i, 0))
    return pl.pallas_call(
        _merge_kernel,
        grid=(T // tm,),
        in_specs=[spec] * 6,
        out_specs=spec,
        out_shape=jax.ShapeDtypeStruct((T, w), BF16),
        compiler_params=_cparams("parallel"),
        name="dil_merge",
    )(*os_, *ls_)


def _out_kernel(oa_ref, ob_ref, oc_ref, od_ref, w_ref, x_ref, gt_ref, o_ref):
    acc = _dot(oa_ref[...], w_ref[0:512, :])
    acc += _dot(ob_ref[...], w_ref[512:1024, :])
    acc += _dot(oc_ref[...], w_ref[1024:1536, :])
    acc += _dot(od_ref[...], w_ref[1536:2048, :])
    o_ref[...] = x_ref[...] + gt_ref[...] * acc


def _out_proj(oa, ob, oc, od, w, x, mod, S):
    T, D = x.shape
    tm = 512
    mix = pl.BlockSpec((tm, 512), lambda i: (i, 0))
    return pl.pallas_call(
        _out_kernel,
        grid=(T // tm,),
        in_specs=[mix, mix, mix, mix,
                  pl.BlockSpec(w.shape, lambda i: (0, 0)),
                  pl.BlockSpec((tm, D), lambda i: (i, 0)),
                  _mod_spec(2, tm, S)],
        out_specs=pl.BlockSpec((tm, D), lambda i: (i, 0)),
        out_shape=jax.ShapeDtypeStruct((T, D), F32),
        compiler_params=_cparams("parallel"),
        name="out_proj",
    )(oa, ob, oc, od, w, x, mod)


def _swiglu_tile(h, wg, wu, wd):
    g = _dot(h, wg)
    u = _dot(h, wu)
    return _dot((g * jax.nn.sigmoid(g) * u).astype(BF16), wd)


def _ffn_kernel(x_ref, g_ref, sh_ref, sc_ref, gt_ref, wg_ref, wu_ref, wd_ref, o_ref, hn_ref, acc_ref):
    j = pl.program_id(1)

    @pl.when(j == 0)
    def _():
        hn_ref[...] = _norm_mod(x_ref[...], g_ref[...], sh_ref[...], sc_ref[...]).astype(BF16)
        acc_ref[...] = jnp.zeros_like(acc_ref)

    acc_ref[...] += _swiglu_tile(hn_ref[...], wg_ref[...], wu_ref[...], wd_ref[...])

    @pl.when(j == pl.num_programs(1) - 1)
    def _():
        o_ref[...] = x_ref[...] + gt_ref[...] * acc_ref[...]


def _ffn(x, g, mod, wg, wu, wd, S):
    T, D = x.shape
    F = wg.shape[1]
    tm, tf = 512, 512
    return pl.pallas_call(
        _ffn_kernel,
        grid=(T // tm, F // tf),
        in_specs=[
            pl.BlockSpec((tm, D), lambda i, j: (i, 0)),
            pl.BlockSpec((1, D), lambda i, j: (0, 0)),
            _mod_spec(3, tm, S),
            _mod_spec(4, tm, S),
            _mod_spec(5, tm, S),
            pl.BlockSpec((D, tf), lambda i, j: (0, j)),
            pl.BlockSpec((D, tf), lambda i, j: (0, j)),
            pl.BlockSpec((tf, D), lambda i, j: (j, 0)),
        ],
        out_specs=pl.BlockSpec((tm, D), lambda i, j: (i, 0)),
        out_shape=jax.ShapeDtypeStruct((T, D), F32),
        scratch_shapes=[pltpu.VMEM((tm, D), BF16), pltpu.VMEM((tm, D), F32)],
        compiler_params=_cparams("parallel", "arbitrary"),
        name="ffn",
    )(x, g.reshape(1, D), mod, mod, mod, wg, wu, wd)


def _router_kernel(x_ref, g_ref, sh_ref, sc_ref, wr_ref, hn_ref, comb_ref):
    h = _norm_mod(x_ref[...], g_ref[...], sh_ref[...], sc_ref[...])
    hn_ref[...] = h.astype(BF16)
    logits = jnp.dot(h, wr_ref[...], preferred_element_type=F32, precision=lax.Precision.HIGHEST)
    lane = lax.broadcasted_iota(jnp.int32, logits.shape, 1)
    logits = jnp.where(lane < N_EXPERTS, logits, NEG)
    m1 = jnp.max(logits, axis=-1, keepdims=True)
    lanef = lane.astype(F32)
    i1 = jnp.min(jnp.where(logits == m1, lanef, float(LANES)), axis=-1, keepdims=True)
    rest = jnp.where(lanef == i1, NEG, logits)
    m2 = jnp.max(rest, axis=-1, keepdims=True)
    i2 = jnp.min(jnp.where(rest == m2, lanef, float(LANES)), axis=-1, keepdims=True)
    e2 = jnp.exp(m2 - m1)
    g1 = 1.0 / (1.0 + e2)
    g2 = e2 / (1.0 + e2)
    comb_ref[...] = jnp.where(lanef == i1, g1, 0.0) + jnp.where(lanef == i2, g2, 0.0)


def _router(x, g, mod, wr_pad, S):
    T, D = x.shape
    tm = 512
    return pl.pallas_call(
        _router_kernel,
        grid=(T // tm,),
        in_specs=[
            pl.BlockSpec((tm, D), lambda i: (i, 0)),
            pl.BlockSpec((1, D), lambda i: (0, 0)),
            _mod_spec(3, tm, S),
            _mod_spec(4, tm, S),
            pl.BlockSpec((D, LANES), lambda i: (0, 0)),
        ],
        out_specs=[pl.BlockSpec((tm, D), lambda i: (i, 0)), pl.BlockSpec((tm, LANES), lambda i: (i, 0))],
        out_shape=[jax.ShapeDtypeStruct((T, D), BF16), jax.ShapeDtypeStruct((T, LANES), F32)],
        compiler_params=_cparams("parallel"),
        name="router",
    )(x, g.reshape(1, D), mod, mod, wr_pad)


def _moe_kernel(hn_ref, comb_ref, x_ref, gt_ref, wg_ref, wu_ref, wd_ref, o_ref, acc_ref):
    e = pl.program_id(1)
    j = pl.program_id(2)

    @pl.when((e == 0) & (j == 0))
    def _():
        acc_ref[...] = jnp.zeros_like(acc_ref)

    comb = comb_ref[...]
    lane = lax.broadcasted_iota(jnp.int32, comb.shape, 1)
    ce = jnp.sum(jnp.where(lane == e, comb, 0.0), axis=-1, keepdims=True)
    acc_ref[...] += ce * _swiglu_tile(hn_ref[...], wg_ref[...], wu_ref[...], wd_ref[...])

    @pl.when((e == pl.num_programs(1) - 1) & (j == pl.num_programs(2) - 1))
    def _():
        o_ref[...] = x_ref[...] + gt_ref[...] * acc_ref[...]


def _moe(hn, comb, x, mod, wg, wu, wd, S):
    T, D = x.shape
    E, _, F = wg.shape
    tm, tf = 512, 512
    return pl.pallas_call(
        _moe_kernel,
        grid=(T // tm, E, F // tf),
        in_specs=[
            pl.BlockSpec((tm, D), lambda i, e, j: (i, 0)),
            pl.BlockSpec((tm, LANES), lambda i, e, j: (i, 0)),
            pl.BlockSpec((tm, D), lambda i, e, j: (i, 0)),
            _mod_spec(5, tm, S),
            pl.BlockSpec((None, D, tf), lambda i, e, j: (e, 0, j)),
            pl.BlockSpec((None, D, tf), lambda i, e, j: (e, 0, j)),
            pl.BlockSpec((None, tf, D), lambda i, e, j: (e, j, 0)),
        ],
        out_specs=pl.BlockSpec((tm, D), lambda i, e, j: (i, 0)),
        out_shape=jax.ShapeDtypeStruct((T, D), F32),
        scratch_shapes=[pltpu.VMEM((tm, D), F32)],
        compiler_params=_cparams("parallel", "arbitrary", "arbitrary"),
        name="moe",
    )(hn, comb, x, mod, wg, wu, wd)


def _final_kernel(x_ref, g_ref, o_ref):
    x = x_ref[...]
    o_ref[...] = x * lax.rsqrt(jnp.mean(x * x, axis=-1, keepdims=True) + NORM_EPS) * g_ref[...]


def _final_norm(x, g):
    T, D = x.shape
    tm = 1024
    return pl.pallas_call(
        _final_kernel,
        grid=(T // tm,),
        in_specs=[pl.BlockSpec((tm, D), lambda i: (i, 0)), pl.BlockSpec((1, D), lambda i: (0, 0))],
        out_specs=pl.BlockSpec((tm, D), lambda i: (i, 0)),
        out_shape=jax.ShapeDtypeStruct((T, D), F32),
        compiler_params=_cparams("parallel"),
        name="final_norm",
    )(x, g.reshape(1, D))


def _prep_layer(l, p):
    t5 = p["t5_table"]
    sw_tab = t5[:, :SW_HEADS].T[jnp.asarray(SW_PERM)]
    dil_tab = t5[:, SW_HEADS:].reshape(T5_BUCKETS, DIL_GROUPS, DIL_HEADS).transpose(1, 2, 0)
    out = {
        "w_in": (p["w_in"][l][:, _IN_SRC] * _IN_MUL).astype(BF16),
        "w_uq": (p["mla_w_uq"][l][:, _UQ_SRC] * _UQ_MUL).astype(BF16),
        "w_ukv": p["mla_w_ukv"][l][:, _UKV_SRC].astype(BF16),
        "w_out": p["w_out"][l][_OUT_SRC].astype(BF16),
        "na_bias": _na_bias(p["na_rpb"][l]),
        "sw_bias": _band_bias(sw_tab, SW_RADIUS, 1, LANES, LANES + 2 * SW_RADIUS),
        "sinks": p["sw_sinks"][l][jnp.asarray(SW_PERM)],
        "dil_tab": dil_tab,
    }
    if l % 2 == 0:
        out["wg"] = p["ffn_w_gate"][l // 2].astype(BF16)
        out["wu"] = p["ffn_w_up"][l // 2].astype(BF16)
        out["wd"] = p["ffn_w_down"][l // 2].astype(BF16)
    else:
        out["wr"] = jnp.pad(p["moe_w_router"][l // 2], ((0, 0), (0, LANES - N_EXPERTS)))
        out["wg"] = p["moe_w_gate"][l // 2].astype(BF16)
        out["wu"] = p["moe_w_up"][l // 2].astype(BF16)
        out["wd"] = p["moe_w_down"][l // 2].astype(BF16)
    return out


def _token_mix(proj, lw, p, l, B, S, rope):
    o_a = _na_attn(proj, lw["na_bias"], B, S)
    qm, km, vm = _mla_up(proj, rope[0], rope[1], p["mla_g_q"][l], p["mla_g_kv"][l], lw["w_uq"], lw["w_ukv"], S)
    o_b = _mla_attn(qm, km, vm, B, S)
    o_c = _band_attn(proj, lw["sw_bias"], B=B, S=S, dil=1, q_col=P_SWQ, k_col=P_SWK, v_col=P_SWV,
                     radius=SW_RADIUS, shared_kv=True, sinks=lw["sinks"])
    os_, ls_ = [], []
    gw = DIL_HEADS * HEAD_DIM
    for g, dil in enumerate(DIL_DILATIONS):
        L = S // dil
        kw = min(L, LANES + 2 * DIL_SIDE)
        bias = _band_bias(lw["dil_tab"][g], DIL_SIDE, dil, LANES, kw)
        o, lse = _band_attn(proj, bias, B=B, S=S, dil=dil, q_col=P_DIL + g * gw,
                            k_col=P_DIL + (DIL_GROUPS + g) * gw, v_col=P_DIL + (2 * DIL_GROUPS + g) * gw,
                            radius=DIL_SIDE, shared_kv=False, with_lse=True)
        os_.append(o)
        ls_.append(lse)
    o_d = _merge(os_, ls_)
    return o_a, o_b, o_c, o_d


def _run_trunk(x, mods, layer_w, p, B, S):
    T = B * S
    x = x.reshape(T, D_MODEL)
    rope = _rope_tables(S)
    for l in range(DEPTH):
        lw = layer_w[l]
        mod = mods[l]
        proj = _in_proj(x, p["g_mix"][l], mod, lw["w_in"], S)
        o_a, o_b, o_c, o_d = _token_mix(proj, lw, p, l, B, S, rope)
        x = _out_proj(o_a, o_b, o_c, o_d, lw["w_out"], x, mod, S)
        if l % 2 == 0:
            x = _ffn(x, p["g_ffn"][l], mod, lw["wg"], lw["wu"], lw["wd"], S)
        else:
            hn, comb = _router(x, p["g_ffn"][l], mod, lw["wr"], S)
            x = _moe(hn, comb, x, mod, lw["wg"], lw["wu"], lw["wd"], S)
    return _final_norm(x, p["g_final"]).reshape(B, S, D_MODEL)


def kernel(x_prompt, x_sample, c_prompt, c_sample, w_ada, b_ada, g_mix, g_ffn, w_in, mla_g_q, mla_g_kv, mla_w_uq, mla_w_ukv, na_rpb, sw_sinks, t5_table, w_out, ffn_w_gate, ffn_w_up, ffn_w_down, moe_w_router, moe_w_gate, moe_w_up, moe_w_down, g_final):
    p = dict(g_mix=g_mix, g_ffn=g_ffn, w_in=w_in, mla_g_q=mla_g_q, mla_g_kv=mla_g_kv, mla_w_uq=mla_w_uq,
             mla_w_ukv=mla_w_ukv, na_rpb=na_rpb, sw_sinks=sw_sinks, t5_table=t5_table, w_out=w_out,
             ffn_w_gate=ffn_w_gate, ffn_w_up=ffn_w_up, ffn_w_down=ffn_w_down, moe_w_router=moe_w_router,
             moe_w_gate=moe_w_gate, moe_w_up=moe_w_up, moe_w_down=moe_w_down, g_final=g_final)
    Bp, Sp, _ = x_prompt.shape
    Bs, Ss, _ = x_sample.shape
    rows = 16
    c_pad = jnp.concatenate([c_prompt, c_sample, jnp.zeros((rows - Bp - Bs, D_MODEL), F32)], axis=0)
    mod_all = _ada(c_pad, w_ada, b_ada)
    mods_p = [mod_all[l, :Bp].reshape(Bp, 6, 1, D_MODEL) for l in range(DEPTH)]
    mods_s = [mod_all[l, Bp:Bp + Bs].reshape(Bs, 6, 1, D_MODEL) for l in range(DEPTH)]
    layer_w = [_prep_layer(l, p) for l in range(DEPTH)]
    y_prompt = _run_trunk(x_prompt, mods_p, layer_w, p, Bp, Sp)
    y_sample = _run_trunk(x_sample, mods_s, layer_w, p, Bs, Ss)
    return (y_prompt, y_sample)
```

```python
import functools
import math

import numpy as np
import jax
import jax.numpy as jnp
from jax import lax
from jax.experimental import pallas as pl
from jax.experimental.pallas import tpu as pltpu

F32 = jnp.float32
BF16 = jnp.bfloat16

D_MODEL = 2048
DEPTH = 4
HEAD_DIM = 64
GRID_W = 64
NA_HEADS = 8
NA_ROWS = 8
NA_COLS = 16
MLA_HEADS = 8
MLA_Q_RANK = 512
MLA_KV_RANK = 256
MLA_NOPE = 64
MLA_ROPE = 32
MLA_V = 64
ROPE_THETA = 10000.0
SW_HEADS = 8
SW_KV_HEADS = 2
SW_RADIUS = 128
DIL_DILATIONS = (1, 4, 16)
DIL_GROUPS = 3
DIL_HEADS = 8
DIL_SIDE = 64
T5_BUCKETS = 32
T5_MAX_DIST = 1024
N_EXPERTS = 8
D_FF = 5632
NORM_EPS = 1e-6

LANES = 128
NEG = -1e30
VMEM_LIMIT = 56 * 1024 * 1024
MOE_TM = 512

NA_IN = 3 * NA_HEADS * HEAD_DIM
MLA_IN = MLA_Q_RANK + MLA_KV_RANK + MLA_ROPE
SW_IN = (SW_HEADS + 2 * SW_KV_HEADS) * HEAD_DIM
DIL_IN = 3 * DIL_GROUPS * DIL_HEADS * HEAD_DIM
P_NA = 0
P_CQ = 1536
P_CKV = 2048
P_KR = 2304
P_KRR = 2432
P_SWQ = 2560
P_SWK = 3072
P_SWV = 3200
P_MAIN = 3584
GW = DIL_HEADS * HEAD_DIM
SW_PERM = (0, 4, 1, 5, 2, 6, 3, 7)


def _cparams(*sem):
    return pltpu.CompilerParams(dimension_semantics=sem, vmem_limit_bytes=VMEM_LIMIT)


def _nt_dot(a, b):
    return lax.dot_general(a, b, (((1,), (1,)), ((), ())), preferred_element_type=F32)


def _dot(a, b):
    return jnp.dot(a, b, preferred_element_type=F32)


def _rot_half_cols(base):
    half = MLA_ROPE // 2
    src = [base + half + j for j in range(half)] + [base + j for j in range(half)]
    sgn = [-1.0] * half + [1.0] * half
    return src, sgn


def _main_colmap():
    src = np.zeros((P_MAIN,), np.int32)
    mul = np.zeros((P_MAIN,), np.float32)
    qs = HEAD_DIM ** -0.5
    for j in range(NA_IN):
        src[P_NA + j] = j
        mul[P_NA + j] = qs if j < NA_HEADS * HEAD_DIM else 1.0
    b0 = NA_IN
    for j in range(MLA_Q_RANK + MLA_KV_RANK):
        src[P_CQ + j] = b0 + j
        mul[P_CQ + j] = 1.0
    kr0 = b0 + MLA_Q_RANK + MLA_KV_RANK
    rsrc, rsgn = _rot_half_cols(kr0)
    for rep in range(2):
        for j in range(MLA_ROPE):
            src[P_KR + rep * MLA_ROPE + j] = kr0 + j
            mul[P_KR + rep * MLA_ROPE + j] = 1.0
            src[P_KRR + rep * MLA_ROPE + j] = rsrc[j]
            mul[P_KRR + rep * MLA_ROPE + j] = rsgn[j]
    c0 = NA_IN + MLA_IN
    for hh, h in enumerate(SW_PERM):
        for d in range(HEAD_DIM):
            src[P_SWQ + hh * HEAD_DIM + d] = c0 + h * HEAD_DIM + d
            mul[P_SWQ + hh * HEAD_DIM + d] = qs
    for j in range(2 * SW_KV_HEADS * HEAD_DIM):
        src[P_SWK + j] = c0 + SW_HEADS * HEAD_DIM + j
        mul[P_SWK + j] = 1.0
    return src, mul


def _dil_colmap(g):
    d0 = NA_IN + MLA_IN + SW_IN
    src = np.zeros((3 * GW,), np.int32)
    mul = np.ones((3 * GW,), np.float32)
    for t in range(3):
        for j in range(GW):
            src[t * GW + j] = d0 + (t * DIL_GROUPS + g) * GW + j
    mul[:GW] = HEAD_DIM ** -0.5
    return src, mul


def _mla_q_colmap():
    n = 3 * 4 * LANES
    src = np.zeros((n,), np.int32)
    mul = np.zeros((n,), np.float32)
    hw = MLA_NOPE + MLA_ROPE
    for h in range(MLA_HEADS):
        p, a = divmod(h, 2)
        for d in range(MLA_NOPE):
            src[p * LANES + a * MLA_NOPE + d] = h * hw + d
            mul[p * LANES + a * MLA_NOPE + d] = 1.0
        rsrc, rsgn = _rot_half_cols(h * hw + MLA_NOPE)
        for j in range(MLA_ROPE):
            ca = 4 * LANES + p * LANES + a * MLA_ROPE + j
            cb = 8 * LANES + p * LANES + a * MLA_ROPE + j
            src[ca] = h * hw + MLA_NOPE + j
            mul[ca] = 1.0
            src[cb] = rsrc[j]
            mul[cb] = rsgn[j]
    return src, mul


def _mla_kv_colmap():
    n = 2 * 4 * LANES
    src = np.zeros((n,), np.int32)
    hw = MLA_NOPE + MLA_V
    for h in range(MLA_HEADS):
        for d in range(MLA_NOPE):
            src[h * MLA_NOPE + d] = h * hw + d
            src[4 * LANES + h * MLA_V + d] = h * hw + MLA_NOPE + d
    return src, np.ones((n,), np.float32)


def _out_proj_rowmap():
    src = np.arange(4 * 512, dtype=np.int32)
    for hh, h in enumerate(SW_PERM):
        for d in range(HEAD_DIM):
            src[1024 + hh * HEAD_DIM + d] = 1024 + h * HEAD_DIM + d
    return src, np.ones((4 * 512,), np.float32)


def _runs(colmap):
    src, mul = colmap
    runs, i, n = [], 0, len(src)
    while i < n:
        j = i + 1
        if mul[i] == 0.0:
            while j < n and mul[j] == 0.0:
                j += 1
            runs.append((None, j - i, 0.0))
        else:
            while j < n and mul[j] == mul[i] and src[j] == src[j - 1] + 1:
                j += 1
            runs.append((int(src[i]), j - i, float(mul[i])))
        i = j
    return runs


_MAIN_RUNS = _runs(_main_colmap())
_DIL_RUNS = [_runs(_dil_colmap(g)) for g in range(DIL_GROUPS)]
_UQ_RUNS = _runs(_mla_q_colmap())
_UKV_RUNS = _runs(_mla_kv_colmap())
_OUT_RUNS = _runs(_out_proj_rowmap())


def _take(w, runs, axis):
    parts = []
    for start, n, m in runs:
        if start is None:
            shape = list(w.shape)
            shape[axis] = n
            parts.append(jnp.zeros(shape, w.dtype))
        else:
            piece = lax.slice_in_dim(w, start, start + n, axis=axis)
            parts.append(piece if m == 1.0 else piece * m)
    return jnp.concatenate(parts, axis=axis).astype(BF16)


def _t5_bucket(rel):
    half = T5_BUCKETS // 2
    max_exact = half // 2
    side = jnp.where(rel > 0, half, 0)
    n = jnp.abs(rel)
    nf = jnp.maximum(n, 1).astype(F32)
    large = max_exact + (jnp.log(nf / max_exact) / math.log(T5_MAX_DIST / max_exact) * (half - max_exact)).astype(jnp.int32)
    large = jnp.minimum(large, half - 1)
    return side + jnp.where(n < max_exact, n, large)


def _pair_rows(b):
    H, V, q, k = b.shape
    return b.reshape(H // 2, 2, V, q, k).transpose(0, 2, 1, 3, 4).reshape(H // 2, V, 2 * q, k)


def _band_bias(table_hb, radius, dilation, tq, kw):
    qa = np.arange(tq)[:, None]
    kc = np.arange(kw)[None, :]
    outs = []
    for v in range(3):
        rel = kc - qa - v * radius
        valid = jnp.asarray(np.abs(rel) <= radius)
        b = table_hb[:, _t5_bucket(jnp.asarray(rel * dilation, jnp.int32))]
        outs.append(jnp.where(valid[None], b, NEG))
    return _pair_rows(jnp.stack(outs, axis=1).astype(F32))


def _na_bias(rpb):
    c = np.arange(GRID_W)[:, None]
    kc = np.arange(GRID_W)[None, :]
    col_start = np.clip(c - NA_COLS // 2, 0, GRID_W - NA_COLS)
    valid = (kc >= col_start) & (kc < col_start + NA_COLS)
    cidx = np.clip(kc - c + NA_COLS - 1, 0, 2 * NA_COLS - 2)
    d = np.arange(NA_ROWS)[:, None]
    kr = np.arange(NA_ROWS)[None, :]
    ridx = kr - d + NA_ROWS - 1
    b = rpb[:, ridx[:, :, None, None], cidx[None, None, :, :]]
    b = jnp.where(jnp.asarray(valid)[None, None, None], b, NEG)
    b = b.transpose(0, 1, 3, 2, 4)
    return _pair_rows(b.reshape(NA_HEADS, NA_ROWS, GRID_W, NA_ROWS * GRID_W).astype(F32))


def _rope_tables(S):
    inv = jnp.power(jnp.float32(ROPE_THETA), -jnp.arange(0, MLA_ROPE, 2, dtype=F32) / MLA_ROPE)
    ang = jnp.arange(S, dtype=F32)[:, None] * inv[None, :]
    cos, sin = jnp.cos(ang), jnp.sin(ang)
    z = jnp.zeros((S, LANES - 2 * MLA_ROPE), F32)
    cos_l = jnp.concatenate([cos, cos, cos, cos, z], axis=1)
    sin_l = jnp.concatenate([sin, sin, sin, sin, z], axis=1)
    return cos_l, sin_l


def _ada_kernel(c_ref, w_ref, b_ref, o_ref):
    c = c_ref[...]
    cs = (c * jax.nn.sigmoid(c)).astype(BF16)
    o_ref[...] = _dot(cs, w_ref[...].astype(BF16)) + b_ref[...]


def _ada(c_pad, w_ada, b_ada):
    L, D, N = w_ada.shape
    R = c_pad.shape[0]
    tn = 1024
    return pl.pallas_call(
        _ada_kernel,
        grid=(L, N // tn),
        in_specs=[
            pl.BlockSpec((R, D), lambda l, j: (0, 0)),
            pl.BlockSpec((None, D, tn), lambda l, j: (l, 0, j)),
            pl.BlockSpec((None, 1, tn), lambda l, j: (l, 0, j)),
        ],
        out_specs=pl.BlockSpec((None, R, tn), lambda l, j: (l, 0, j)),
        out_shape=jax.ShapeDtypeStruct((L, R, N), F32),
        compiler_params=_cparams("parallel", "parallel"),
        name="ada",
    )(c_pad, w_ada, b_ada.reshape(L, 1, N))


def _norm_mod(x, g, sh, sc):
    y = x * lax.rsqrt(jnp.mean(x * x, axis=-1, keepdims=True) + NORM_EPS) * g
    return y * (1.0 + sc) + sh


def _mod_spec(k, tm, S):
    return pl.BlockSpec((None, None, 1, D_MODEL), lambda i, *_: ((i * tm) // S, k, 0, 0))


def _lane_half_mask(shape):
    return lax.broadcasted_iota(jnp.int32, shape, len(shape) - 1) < HEAD_DIM


def _stack_heads(q, first):
    zero = jnp.zeros_like(q)
    return jnp.concatenate([jnp.where(first, q, zero), jnp.where(first, zero, q)], axis=0)


def _in_kernel(x_ref, g_ref, sh_ref, sc_ref, w_ref, o_ref, hn_ref):
    @pl.when(pl.program_id(1) == 0)
    def _():
        hn_ref[...] = _norm_mod(x_ref[...], g_ref[...], sh_ref[...], sc_ref[...]).astype(BF16)

    o_ref[...] = _dot(hn_ref[...], w_ref[...]).astype(BF16)


def _in_proj(x, g, mod, w, S):
    T, D = x.shape
    N = w.shape[1]
    tm, tn = 512, N // 2
    return pl.pallas_call(
        _in_kernel,
        grid=(T // tm, N // tn),
        in_specs=[
            pl.BlockSpec((tm, D), lambda i, j: (i, 0)),
            pl.BlockSpec((1, D), lambda i, j: (0, 0)),
            _mod_spec(0, tm, S),
            _mod_spec(1, tm, S),
            pl.BlockSpec((D, tn), lambda i, j: (0, j)),
        ],
        out_specs=[pl.BlockSpec((tm, tn), lambda i, j: (i, j)),
                   pl.BlockSpec((tm, D), lambda i, j: (i, 0))],
        out_shape=[jax.ShapeDtypeStruct((T, N), BF16), jax.ShapeDtypeStruct((T, D), BF16)],
        compiler_params=_cparams("parallel", "arbitrary"),
        name="in_proj",
    )(x, g.reshape(1, D), mod, mod, w)


def _dil_proj_kernel(h_ref, w_ref, o_ref, *scratch, dil):
    r = _dot(h_ref[...], w_ref[...])
    if dil == 1:
        o_ref[0] = r.astype(BF16)
    else:
        r_ref, = scratch
        n = r_ref.shape[1] // dil
        for jb in range(r_ref.shape[0]):
            cs = slice(jb * LANES, (jb + 1) * LANES)
            r_ref[jb] = r[:, cs]
            for c in range(dil):
                o_ref[c, :, cs] = r_ref[jb, pl.ds(c, n, stride=dil), :].astype(BF16)


def _dil_proj(hn, w, B, S, dil):
    T, D = hn.shape
    N = w.shape[1]
    tm = 1024
    nsb = S // tm
    scratch = [] if dil == 1 else [pltpu.VMEM((N // LANES, tm, LANES), F32)]
    return pl.pallas_call(
        functools.partial(_dil_proj_kernel, dil=dil),
        grid=(T // tm,),
        in_specs=[pl.BlockSpec((tm, D), lambda i: (i, 0)), pl.BlockSpec((D, N), lambda i: (0, 0))],
        out_specs=pl.BlockSpec((None, dil, tm // dil, N), lambda i: (i // nsb, 0, i % nsb, 0)),
        out_shape=jax.ShapeDtypeStruct((B, dil, S // dil, N), BF16),
        scratch_shapes=scratch,
        compiler_params=_cparams("parallel"),
        name="dil_proj_d%d" % dil,
    )(hn, w)


def _softmax_pv(s, v):
    m = jnp.max(s, axis=-1, keepdims=True)
    p = jnp.exp(s - m)
    l = jnp.sum(p, axis=-1, keepdims=True)
    return _dot(p.astype(BF16), v) / l


def _na_kernel(q_ref, k_ref, v_ref, bias_ref, o_ref, *, rows, qrows):
    rb = pl.program_id(1)
    kwin = NA_ROWS * GRID_W
    lo = _lane_half_mask((GRID_W, LANES))
    for qr in range(qrows):
        r = rb * qrows + qr
        r_start = jnp.clip(r - NA_ROWS // 2, 0, rows - NA_ROWS)
        d = r - r_start
        k0 = pl.multiple_of(r_start * GRID_W, GRID_W)
        for p in range(NA_HEADS // 2):
            cs = slice(p * LANES, (p + 1) * LANES)
            q = q_ref[qr * GRID_W:(qr + 1) * GRID_W, cs]
            k = k_ref[pl.ds(k0, kwin), cs]
            v = v_ref[pl.ds(k0, kwin), cs]
            o = _softmax_pv(_nt_dot(_stack_heads(q, lo), k) + bias_ref[p, d], v)
            o_ref[qr * GRID_W:(qr + 1) * GRID_W, cs] = jnp.where(lo, o[:GRID_W], o[GRID_W:]).astype(BF16)


def _na_attn(proj, bias, B, S):
    T = B * S
    rows = S // GRID_W
    qrows = 8
    tq = qrows * GRID_W
    nqb = S // tq
    w = NA_HEADS * HEAD_DIM
    kern = functools.partial(_na_kernel, rows=rows, qrows=qrows)
    return pl.pallas_call(
        kern,
        grid=(B, nqb),
        in_specs=[
            pl.BlockSpec((tq, w), lambda b, r: (b * nqb + r, P_NA // w)),
            pl.BlockSpec((S, w), lambda b, r: (b, P_NA // w + 1)),
            pl.BlockSpec((S, w), lambda b, r: (b, P_NA // w + 2)),
            pl.BlockSpec(bias.shape, lambda b, r: (0, 0, 0, 0)),
        ],
        out_specs=pl.BlockSpec((tq, w), lambda b, r: (b * nqb + r, 0)),
        out_shape=jax.ShapeDtypeStruct((T, w), BF16),
        compiler_params=_cparams("parallel", "arbitrary"),
        name="na_attn",
    )(proj, proj, proj, bias)


def _mla_up_kernel(cq_ref, ckv_ref, kr_ref, krr_ref, cos_ref, sin_ref, gq_ref, gkv_ref, wq_ref, wkv_ref,
                   q_ref, k_ref, v_ref):
    scale = (MLA_NOPE + MLA_ROPE) ** -0.5
    cos = cos_ref[...]
    sin = sin_ref[...]

    def rms(x_ref, g_ref):
        x = x_ref[...].astype(F32)
        return (x * lax.rsqrt(jnp.mean(x * x, axis=-1, keepdims=True) + NORM_EPS) * g_ref[...]).astype(BF16)

    qf = _dot(rms(cq_ref, gq_ref), wq_ref[...]) * scale
    kvf = _dot(rms(ckv_ref, gkv_ref), wkv_ref[...])
    kpe = (kr_ref[...].astype(F32) * cos + krr_ref[...].astype(F32) * sin).astype(BF16)
    for p in range(MLA_HEADS // 2):
        nope = qf[:, p * LANES:(p + 1) * LANES]
        pe = qf[:, (4 + p) * LANES:(5 + p) * LANES] * cos + qf[:, (8 + p) * LANES:(9 + p) * LANES] * sin
        q_ref[:, 2 * p * LANES:(2 * p + 1) * LANES] = nope.astype(BF16)
        q_ref[:, (2 * p + 1) * LANES:(2 * p + 2) * LANES] = pe.astype(BF16)
        k_ref[:, 2 * p * LANES:(2 * p + 1) * LANES] = kvf[:, p * LANES:(p + 1) * LANES].astype(BF16)
        k_ref[:, (2 * p + 1) * LANES:(2 * p + 2) * LANES] = kpe
    v_ref[...] = kvf[:, 4 * LANES:].astype(BF16)


def _mla_up(proj, cos_l, sin_l, g_q, g_kv, wq, wkv, S):
    T = proj.shape[0]
    tm = 512
    nsb = S // tm
    row = lambda i: (i, 0)
    return pl.pallas_call(
        _mla_up_kernel,
        grid=(T // tm,),
        in_specs=[
            pl.BlockSpec((tm, MLA_Q_RANK), lambda i: (i, P_CQ // MLA_Q_RANK)),
            pl.BlockSpec((tm, MLA_KV_RANK), lambda i: (i, P_CKV // MLA_KV_RANK)),
            pl.BlockSpec((tm, LANES), lambda i: (i, P_KR // LANES)),
            pl.BlockSpec((tm, LANES), lambda i: (i, P_KRR // LANES)),
            pl.BlockSpec((tm, LANES), lambda i: (i % nsb, 0)),
            pl.BlockSpec((tm, LANES), lambda i: (i % nsb, 0)),
            pl.BlockSpec((1, MLA_Q_RANK), lambda i: (0, 0)),
            pl.BlockSpec((1, MLA_KV_RANK), lambda i: (0, 0)),
            pl.BlockSpec(wq.shape, lambda i: (0, 0)),
            pl.BlockSpec(wkv.shape, lambda i: (0, 0)),
        ],
        out_specs=[
            pl.BlockSpec((tm, 8 * LANES), row),
            pl.BlockSpec((tm, 8 * LANES), row),
            pl.BlockSpec((tm, 4 * LANES), row),
        ],
        out_shape=[
            jax.ShapeDtypeStruct((T, 8 * LANES), BF16),
            jax.ShapeDtypeStruct((T, 8 * LANES), BF16),
            jax.ShapeDtypeStruct((T, 4 * LANES), BF16),
        ],
        compiler_params=_cparams("parallel"),
        name="mla_up",
    )(proj, proj, proj, proj, cos_l, sin_l, g_q.reshape(1, -1), g_kv.reshape(1, -1), wq, wkv)


def _mla_attn_kernel(q_ref, k_ref, v_ref, o_ref, vt_ref, *, tk):
    tq = q_ref.shape[0]
    nk = vt_ref.shape[0]

    @pl.when(pl.program_id(2) == 0)
    def _():
        for c in range(nk):
            vt_ref[c] = v_ref[c * tk:(c + 1) * tk, :].astype(F32).T.astype(BF16)

    q = q_ref[...]
    lane = lax.broadcasted_iota(jnp.int32, q.shape, 1)
    first = (lane < MLA_NOPE) | ((lane >= LANES) & (lane < LANES + MLA_ROPE))
    qcat = _stack_heads(q, first)

    def scores(c):
        return _nt_dot(k_ref[c * tk:(c + 1) * tk, :], qcat)

    m = jnp.full((1, 2 * tq), NEG, F32)
    l = jnp.zeros((1, 2 * tq), F32)
    acc = jnp.zeros((LANES, 2 * tq), F32)
    st = scores(0)
    for c in range(nk):
        st_next = scores(c + 1) if c + 1 < nk else None
        mn = jnp.maximum(m, jnp.max(st, axis=0, keepdims=True))
        al = jnp.exp(m - mn)
        p = jnp.exp(st - mn)
        l = al * l + jnp.sum(p, axis=0, keepdims=True)
        acc = al * acc + _dot(vt_ref[c], p.astype(BF16))
        m, st = mn, st_next
    o = (acc / l).T
    lo = _lane_half_mask((tq, LANES))
    o_ref[...] = jnp.where(lo, o[:tq], o[tq:]).astype(BF16)


def _mla_attn(qm, km, vm, B, S):
    T = B * S
    tq, tk = 256, 256
    nqb = S // tq
    kern = functools.partial(_mla_attn_kernel, tk=tk)
    return pl.pallas_call(
        kern,
        grid=(B, MLA_HEADS // 2, nqb),
        in_specs=[
            pl.BlockSpec((tq, 2 * LANES), lambda b, p, i: (b * nqb + i, p)),
            pl.BlockSpec((S, 2 * LANES), lambda b, p, i: (b, p)),
            pl.BlockSpec((S, LANES), lambda b, p, i: (b, p)),
        ],
        out_specs=pl.BlockSpec((tq, LANES), lambda b, p, i: (b * nqb + i, p)),
        out_shape=jax.ShapeDtypeStruct((T, MLA_HEADS * MLA_V), BF16),
        scratch_shapes=[pltpu.VMEM((S // tk, LANES, tk), BF16)],
        compiler_params=_cparams("parallel", "parallel", "arbitrary"),
        name="mla_attn",
    )(qm, km, vm)


def _band_kernel(*refs, L, radius, tq, kw, n_pairs, shared_kv, with_sink, with_lse):
    if with_sink:
        sink_ref, refs = refs[0], refs[1:]
    q_ref, k_ref, v_ref, bias_ref = refs[:4]
    o_ref = refs[4]
    lse_ref = refs[5] if with_lse else None
    qb = pl.program_id(2)
    nq = q_ref.shape[0] // tq
    lo = _lane_half_mask((tq, LANES))
    second = lax.broadcasted_iota(jnp.int32, (2 * tq, 1), 0) >= tq

    def q_tile(t, carry):
        q0 = (qb * nq + t) * tq
        start = jnp.clip(q0 - radius, 0, L - kw)
        var = (q0 - start) // radius
        k0 = pl.multiple_of(start, min(radius, tq))
        r0 = pl.multiple_of(t * tq, tq)
        for p in range(n_pairs):
            cs = slice(p * LANES, (p + 1) * LANES)
            kcs = slice(0, LANES) if shared_kv else cs
            q = q_ref[pl.ds(r0, tq), cs]
            k = k_ref[pl.ds(k0, kw), kcs]
            v = v_ref[pl.ds(k0, kw), kcs]
            s = _nt_dot(_stack_heads(q, lo), k) + bias_ref[p, var]
            m = jnp.max(s, axis=-1, keepdims=True)
            if with_sink:
                sk = jnp.where(second, sink_ref[2 * p + 1], sink_ref[2 * p])
                m = jnp.maximum(m, sk)
            e = jnp.exp(s - m)
            l = jnp.sum(e, axis=-1, keepdims=True)
            if with_sink:
                l = l + jnp.exp(sk - m)
            o = _dot(e.astype(BF16), v) / l
            if with_lse:
                lse = jnp.broadcast_to(m + jnp.log(l), (2 * tq, LANES))
                o_ref[pl.ds(r0, tq), cs] = jnp.where(lo, o[:tq], o[tq:])
                lse_ref[pl.ds(r0, tq), cs] = jnp.where(lo, lse[:tq], lse[tq:])
            else:
                o_ref[pl.ds(r0, tq), cs] = jnp.where(lo, o[:tq], o[tq:]).astype(BF16)
        return carry

    lax.fori_loop(0, nq, q_tile, 0)


def _sw_attn(proj, bias, sinks, B, S):
    T = proj.shape[0]
    tq, radius = LANES, SW_RADIUS
    kw = tq + 2 * radius
    qblk = 512
    nqb = S // qblk
    wq = SW_HEADS * HEAD_DIM
    kern = functools.partial(_band_kernel, L=S, radius=radius, tq=tq, kw=kw, n_pairs=wq // LANES,
                             shared_kv=True, with_sink=True, with_lse=False)
    return pl.pallas_call(
        kern,
        grid=(B, 1, nqb),
        in_specs=[
            pl.BlockSpec(memory_space=pltpu.SMEM),
            pl.BlockSpec((qblk, wq), lambda b, c, i: (b * nqb + i, P_SWQ // wq)),
            pl.BlockSpec((S, LANES), lambda b, c, i: (b, P_SWK // LANES)),
            pl.BlockSpec((S, LANES), lambda b, c, i: (b, P_SWV // LANES)),
            pl.BlockSpec(bias.shape, lambda b, c, i: (0, 0, 0, 0)),
        ],
        out_specs=pl.BlockSpec((qblk, wq), lambda b, c, i: (b * nqb + i, 0)),
        out_shape=jax.ShapeDtypeStruct((T, wq), BF16),
        compiler_params=_cparams("parallel", "parallel", "arbitrary"),
        name="band_attn_sw",
    )(sinks, proj, proj, proj, bias)


def _dil_attn(qkv, bias, dil):
    B, _, L, _ = qkv.shape
    tq, radius = LANES, DIL_SIDE
    kw = min(L, tq + 2 * radius)
    qblk = min(L, 512)
    nqb = L // qblk
    kern = functools.partial(_band_kernel, L=L, radius=radius, tq=tq, kw=kw, n_pairs=GW // LANES,
                             shared_kv=False, with_sink=False, with_lse=True)
    o_spec = pl.BlockSpec((None, None, qblk, GW), lambda b, c, i: (b, c, i, 0))
    o_shape = jax.ShapeDtypeStruct((B, dil, L, GW), F32)
    return pl.pallas_call(
        kern,
        grid=(B, dil, nqb),
        in_specs=[
            pl.BlockSpec((None, None, qblk, GW), lambda b, c, i: (b, c, i, 0)),
            pl.BlockSpec((None, None, L, GW), lambda b, c, i: (b, c, 0, 1)),
            pl.BlockSpec((None, None, L, GW), lambda b, c, i: (b, c, 0, 2)),
            pl.BlockSpec(bias.shape, lambda b, c, i: (0, 0, 0, 0)),
        ],
        out_specs=[o_spec, o_spec],
        out_shape=[o_shape, o_shape],
        compiler_params=_cparams("parallel", "parallel", "arbitrary"),
        name="band_attn_d%d" % dil,
    )(qkv, qkv, qkv, bias)


def _merge_kernel(o0, l0, o1, l1, o2, l2, o_ref, *scratch, dils):
    for jb in range(GW // LANES):
        cs = slice(jb * LANES, (jb + 1) * LANES)
        vals = []
        for g, (o_in, l_in) in enumerate(((o0, l0), (o1, l1), (o2, l2))):
            d = dils[g]
            if d == 1:
                vals.append((o_in[0, :, cs], l_in[0, :, cs]))
            else:
                so, sl = scratch[2 * (g - 1)], scratch[2 * (g - 1) + 1]
                n = so.shape[1] // d
                for c in range(d):
                    so[jb, pl.ds(c, n, stride=d), :] = o_in[c, :, cs]
                    sl[jb, pl.ds(c, n, stride=d), :] = l_in[c, :, cs]
                vals.append((so[jb], sl[jb]))
        (a, la), (b, lb), (c, lc) = vals
        m = jnp.maximum(jnp.maximum(la, lb), lc)
        wa, wb, wc = jnp.exp(la - m), jnp.exp(lb - m), jnp.exp(lc - m)
        o_ref[:, cs] = ((wa * a + wb * b + wc * c) / (wa + wb + wc)).astype(BF16)


def _merge(outs, B, S):
    T = B * S
    tm = 1024
    nsb = S // tm
    in_specs, args = [], []
    for d, (o, l) in zip(DIL_DILATIONS, outs):
        spec = pl.BlockSpec((None, d, tm // d, GW), lambda i: (i // nsb, 0, i % nsb, 0))
        in_specs += [spec, spec]
        args += [o, l]
    scratch = [pltpu.VMEM((GW // LANES, tm, LANES), F32) for d in DIL_DILATIONS[1:] for _ in range(2)]
    return pl.pallas_call(
        functools.partial(_merge_kernel, dils=DIL_DILATIONS),
        grid=(T // tm,),
        in_specs=in_specs,
        out_specs=pl.BlockSpec((tm, GW), lambda i: (i, 0)),
        out_shape=jax.ShapeDtypeStruct((T, GW), BF16),
        scratch_shapes=scratch,
        compiler_params=_cparams("parallel"),
        name="dil_merge",
    )(*args)


def _out_kernel(oa_ref, ob_ref, oc_ref, od_ref, w_ref, x_ref, gt_ref, o_ref):
    acc = _dot(oa_ref[...], w_ref[0:512, :])
    acc += _dot(ob_ref[...], w_ref[512:1024, :])
    acc += _dot(oc_ref[...], w_ref[1024:1536, :])
    acc += _dot(od_ref[...], w_ref[1536:2048, :])
    o_ref[...] = x_ref[...] + gt_ref[...] * acc


def _out_proj(oa, ob, oc, od, w, x, mod, S):
    T, D = x.shape
    tm = 512
    mix = pl.BlockSpec((tm, 512), lambda i: (i, 0))
    return pl.pallas_call(
        _out_kernel,
        grid=(T // tm,),
        in_specs=[mix, mix, mix, mix,
                  pl.BlockSpec(w.shape, lambda i: (0, 0)),
                  pl.BlockSpec((tm, D), lambda i: (i, 0)),
                  _mod_spec(2, tm, S)],
        out_specs=pl.BlockSpec((tm, D), lambda i: (i, 0)),
        out_shape=jax.ShapeDtypeStruct((T, D), F32),
        compiler_params=_cparams("parallel"),
        name="out_proj",
    )(oa, ob, oc, od, w, x, mod)


def _swiglu_tile(h, wg, wu, wd):
    g = _dot(h, wg)
    u = _dot(h, wu)
    return _dot((g * jax.nn.sigmoid(g) * u).astype(BF16), wd)


def _ffn_kernel(x_ref, g_ref, sh_ref, sc_ref, gt_ref, wg_ref, wu_ref, wd_ref, o_ref, hn_ref, acc_ref):
    j = pl.program_id(1)

    @pl.when(j == 0)
    def _():
        hn_ref[...] = _norm_mod(x_ref[...], g_ref[...], sh_ref[...], sc_ref[...]).astype(BF16)
        acc_ref[...] = jnp.zeros_like(acc_ref)

    acc_ref[...] += _swiglu_tile(hn_ref[...], wg_ref[...], wu_ref[...], wd_ref[...])

    @pl.when(j == pl.num_programs(1) - 1)
    def _():
        o_ref[...] = x_ref[...] + gt_ref[...] * acc_ref[...]


def _ffn(x, g, mod, wg, wu, wd, S):
    T, D = x.shape
    F = wg.shape[1]
    tm, tf = 512, 512
    return pl.pallas_call(
        _ffn_kernel,
        grid=(T // tm, F // tf),
        in_specs=[
            pl.BlockSpec((tm, D), lambda i, j: (i, 0)),
            pl.BlockSpec((1, D), lambda i, j: (0, 0)),
            _mod_spec(3, tm, S),
            _mod_spec(4, tm, S),
            _mod_spec(5, tm, S),
            pl.BlockSpec((D, tf), lambda i, j: (0, j)),
            pl.BlockSpec((D, tf), lambda i, j: (0, j)),
            pl.BlockSpec((tf, D), lambda i, j: (j, 0)),
        ],
        out_specs=pl.BlockSpec((tm, D), lambda i, j: (i, 0)),
        out_shape=jax.ShapeDtypeStruct((T, D), F32),
        scratch_shapes=[pltpu.VMEM((tm, D), BF16), pltpu.VMEM((tm, D), F32)],
        compiler_params=_cparams("parallel", "arbitrary"),
        name="ffn",
    )(x, g.reshape(1, D), mod, mod, mod, wg, wu, wd)


SEL_I1, SEL_I2, SEL_G1, SEL_G2, SEL_R1, SEL_R2 = range(6)


def _lane_pick(x, lane, k):
    return jnp.sum(jnp.where(lane == k, x, 0.0), axis=-1, keepdims=True)


def _router_kernel(x_ref, g_ref, sh_ref, sc_ref, wr_ref, hn_ref, sel_ref, cnt_ref):
    @pl.when(pl.program_id(0) == 0)
    def _():
        cnt_ref[...] = jnp.zeros_like(cnt_ref)

    h = _norm_mod(x_ref[...], g_ref[...], sh_ref[...], sc_ref[...])
    hn_ref[...] = h
    tm = h.shape[0]
    logits = jnp.dot(h, wr_ref[...], preferred_element_type=F32, precision=lax.Precision.HIGHEST)
    lane = lax.broadcasted_iota(jnp.int32, logits.shape, 1)
    lanef = lane.astype(F32)
    logits = jnp.where(lane < N_EXPERTS, logits, NEG)
    m1 = jnp.max(logits, axis=-1, keepdims=True)
    i1 = jnp.min(jnp.where(logits == m1, lanef, float(LANES)), axis=-1, keepdims=True)
    rest = jnp.where(lanef == i1, NEG, logits)
    m2 = jnp.max(rest, axis=-1, keepdims=True)
    i2 = jnp.min(jnp.where(rest == m2, lanef, float(LANES)), axis=-1, keepdims=True)
    e2 = jnp.exp(m2 - m1)
    g1 = 1.0 / (1.0 + e2)
    g2 = e2 / (1.0 + e2)
    oh1 = jnp.where(lanef == i1, 1.0, 0.0)
    oh2 = jnp.where(lanef == i2, 1.0, 0.0)
    earlier = (lax.broadcasted_iota(jnp.int32, (tm, tm), 1) < lax.broadcasted_iota(jnp.int32, (tm, tm), 0))
    tri = jnp.where(earlier, 1.0, 0.0).astype(BF16)
    pre1 = _dot(tri, oh1.astype(BF16))
    pre2 = _dot(tri, oh2.astype(BF16))
    tot1 = jnp.sum(oh1, axis=0, keepdims=True)
    tot2 = jnp.sum(oh2, axis=0, keepdims=True)
    seen = cnt_ref[...]
    r1 = jnp.sum(oh1 * (pre1 + seen), axis=-1, keepdims=True)
    r2 = jnp.sum(oh2 * (pre2 + seen + tot1), axis=-1, keepdims=True)
    cnt_ref[...] = seen + tot1 + tot2
    sel = jnp.zeros(logits.shape, F32)
    for k, val in ((SEL_I1, i1), (SEL_I2, i2), (SEL_G1, g1), (SEL_G2, g2), (SEL_R1, r1), (SEL_R2, r2)):
        sel = jnp.where(lane == k, val, sel)
    sel_ref[...] = sel


def _router(x, g, mod, wr_pad, S):
    T, D = x.shape
    tm = 512
    return pl.pallas_call(
        _router_kernel,
        grid=(T // tm,),
        in_specs=[
            pl.BlockSpec((tm, D), lambda i: (i, 0)),
            pl.BlockSpec((1, D), lambda i: (0, 0)),
            _mod_spec(3, tm, S),
            _mod_spec(4, tm, S),
            pl.BlockSpec((D, LANES), lambda i: (0, 0)),
        ],
        out_specs=[pl.BlockSpec((tm, D), lambda i: (i, 0)),
                   pl.BlockSpec((tm, LANES), lambda i: (i, 0)),
                   pl.BlockSpec((1, LANES), lambda i: (0, 0))],
        out_shape=[jax.ShapeDtypeStruct((T, D), F32),
                   jax.ShapeDtypeStruct((T, LANES), F32),
                   jax.ShapeDtypeStruct((1, LANES), F32)],
        compiler_params=_cparams("arbitrary"),
        name="router",
    )(x, g.reshape(1, D), mod, mod, wr_pad)


def _route_tables(sel, cnt, T, n_tiles):
    i1 = sel[:, SEL_I1].astype(jnp.int32)
    i2 = sel[:, SEL_I2].astype(jnp.int32)
    r1 = sel[:, SEL_R1].astype(jnp.int32)
    r2 = sel[:, SEL_R2].astype(jnp.int32)
    counts = cnt[0, :N_EXPERTS].astype(jnp.int32)
    tiles_e = (counts + MOE_TM - 1) // MOE_TM
    ends = jnp.cumsum(tiles_e)
    offs = (ends - tiles_e) * MOE_TM
    pos1 = offs[i1] + r1
    pos2 = offs[i2] + r2
    tile = jnp.arange(n_tiles, dtype=jnp.int32)
    n_used = ends[-1]
    valid = (tile < n_used).astype(jnp.int32)
    expert_of = jnp.sum((tile[:, None] >= ends[None, :]).astype(jnp.int32), axis=1)
    expert_of = expert_of[jnp.minimum(tile, n_used - 1)]
    tok = jnp.arange(T, dtype=jnp.int32)
    src = jnp.zeros((n_tiles * MOE_TM,), jnp.int32).at[pos1].set(tok).at[pos2].set(tok)
    return pos1, pos2, src, expert_of, valid


def _row_copy(src_hbm, row, dst_ref, r, sem):
    return pltpu.make_async_copy(src_hbm.at[pl.ds(row, 1)], dst_ref.at[pl.ds(r, 1)], sem)


def _gather_rows(idx_ref, src_hbm, dst_ref, sem):
    n = dst_ref.shape[0]

    def start(r, c):
        _row_copy(src_hbm, idx_ref[r], dst_ref, r, sem).start()
        return c

    def wait(r, c):
        _row_copy(src_hbm, 0, dst_ref, r, sem).wait()
        return c

    lax.fori_loop(0, n, start, 0)
    lax.fori_loop(0, n, wait, 0)


def _dispatch_kernel(src_ref, h_hbm, o_ref, sem):
    _gather_rows(src_ref, h_hbm, o_ref, sem)


def _dispatch(hn, src):
    T, D = hn.shape
    P = src.shape[0]
    tm = 256
    return pl.pallas_call(
        _dispatch_kernel,
        grid=(P // tm,),
        in_specs=[pl.BlockSpec((tm,), lambda i: (i,), memory_space=pltpu.SMEM),
                  pl.BlockSpec(memory_space=pl.ANY)],
        out_specs=pl.BlockSpec((tm, D), lambda i: (i, 0)),
        out_shape=jax.ShapeDtypeStruct((P, D), F32),
        scratch_shapes=[pltpu.SemaphoreType.DMA(())],
        compiler_params=_cparams("arbitrary"),
        name="moe_dispatch",
    )(src, hn)


def _moe_ffn_kernel(te_ref, va_ref, xs_ref, wg_ref, wu_ref, wd_ref, o_ref, hb_ref, acc_ref):
    i = pl.program_id(0)
    j = pl.program_id(1)
    last = pl.num_programs(1) - 1
    busy = va_ref[i] == 1

    @pl.when(busy & (j == 0))
    def _():
        hb_ref[...] = xs_ref[...].astype(BF16)
        acc_ref[...] = jnp.zeros_like(acc_ref)

    @pl.when(busy)
    def _():
        acc_ref[...] += _swiglu_tile(hb_ref[...], wg_ref[...], wu_ref[...], wd_ref[...])

    @pl.when(busy & (j == last))
    def _():
        o_ref[...] = acc_ref[...]

    @pl.when(jnp.logical_not(busy) & (j == last))
    def _():
        o_ref[...] = jnp.zeros_like(o_ref)


def _moe_ffn(xs, expert_of, valid, wg, wu, wd):
    P, D = xs.shape
    E, _, F = wg.shape
    tm, tf = MOE_TM, 512
    nj = F // tf

    def wcol(i, j, te, va):
        return jnp.where(va[i] == 1, j, nj - 1)

    grid_spec = pltpu.PrefetchScalarGridSpec(
        num_scalar_prefetch=2,
        grid=(P // tm, nj),
        in_specs=[
            pl.BlockSpec((tm, D), lambda i, j, te, va: (i, 0)),
            pl.BlockSpec((None, D, tf), lambda i, j, te, va: (te[i], 0, wcol(i, j, te, va))),
            pl.BlockSpec((None, D, tf), lambda i, j, te, va: (te[i], 0, wcol(i, j, te, va))),
            pl.BlockSpec((None, tf, D), lambda i, j, te, va: (te[i], wcol(i, j, te, va), 0)),
        ],
        out_specs=pl.BlockSpec((tm, D), lambda i, j, te, va: (i, 0)),
        scratch_shapes=[pltpu.VMEM((tm, D), BF16), pltpu.VMEM((tm, D), F32)],
    )
    return pl.pallas_call(
        _moe_ffn_kernel,
        grid_spec=grid_spec,
        out_shape=jax.ShapeDtypeStruct((P, D), F32),
        compiler_params=_cparams("arbitrary", "arbitrary"),
        name="moe_ffn",
    )(expert_of, valid, xs, wg, wu, wd)


def _combine_kernel(p1_ref, p2_ref, sel_ref, x_ref, gt_ref, ys_hbm, o_ref, y1_ref, y2_ref, sem1, sem2):
    _gather_rows(p1_ref, ys_hbm, y1_ref, sem1)
    _gather_rows(p2_ref, ys_hbm, y2_ref, sem2)
    sel = sel_ref[...]
    lane = lax.broadcasted_iota(jnp.int32, sel.shape, 1)
    g1 = _lane_pick(sel, lane, SEL_G1)
    g2 = _lane_pick(sel, lane, SEL_G2)
    o_ref[...] = x_ref[...] + gt_ref[...] * (g1 * y1_ref[...] + g2 * y2_ref[...])


def _combine(ys, pos1, pos2, sel, x, mod, S):
    T, D = x.shape
    tm = 256
    idx = pl.BlockSpec((tm,), lambda i: (i,), memory_space=pltpu.SMEM)
    return pl.pallas_call(
        _combine_kernel,
        grid=(T // tm,),
        in_specs=[idx, idx,
                  pl.BlockSpec((tm, LANES), lambda i: (i, 0)),
                  pl.BlockSpec((tm, D), lambda i: (i, 0)),
                  _mod_spec(5, tm, S),
                  pl.BlockSpec(memory_space=pl.ANY)],
        out_specs=pl.BlockSpec((tm, D), lambda i: (i, 0)),
        out_shape=jax.ShapeDtypeStruct((T, D), F32),
        scratch_shapes=[pltpu.VMEM((tm, D), F32), pltpu.VMEM((tm, D), F32),
                        pltpu.SemaphoreType.DMA(()), pltpu.SemaphoreType.DMA(())],
        compiler_params=_cparams("arbitrary"),
        name="moe_combine",
    )(pos1, pos2, sel, x, mod, ys)


def _moe(x, g, mod, wr_pad, wg, wu, wd, S):
    T = x.shape[0]
    n_tiles = (2 * T) // MOE_TM + N_EXPERTS
    hn, sel, cnt = _router(x, g, mod, wr_pad, S)
    pos1, pos2, src, expert_of, valid = _route_tables(sel, cnt, T, n_tiles)
    xs = _dispatch(hn, src)
    ys = _moe_ffn(xs, expert_of, valid, wg, wu, wd)
    return _combine(ys, pos1, pos2, sel, x, mod, S)


def _final_kernel(x_ref, g_ref, o_ref):
    x = x_ref[...]
    o_ref[...] = x * lax.rsqrt(jnp.mean(x * x, axis=-1, keepdims=True) + NORM_EPS) * g_ref[...]


def _final_norm(x, g):
    T, D = x.shape
    tm = 1024
    return pl.pallas_call(
        _final_kernel,
        grid=(T // tm,),
        in_specs=[pl.BlockSpec((tm, D), lambda i: (i, 0)), pl.BlockSpec((1, D), lambda i: (0, 0))],
        out_specs=pl.BlockSpec((tm, D), lambda i: (i, 0)),
        out_shape=jax.ShapeDtypeStruct((T, D), F32),
        compiler_params=_cparams("parallel"),
        name="final_norm",
    )(x, g.reshape(1, D))


def _prep_layer(l, p):
    t5 = p["t5_table"]
    sw_tab = t5[:, :SW_HEADS].T[jnp.asarray(SW_PERM)]
    dil_tab = t5[:, SW_HEADS:].reshape(T5_BUCKETS, DIL_GROUPS, DIL_HEADS).transpose(1, 2, 0)
    w_in = p["w_in"][l]
    out = {
        "w_main": _take(w_in, _MAIN_RUNS, 1),
        "w_dil": [_take(w_in, _DIL_RUNS[g], 1) for g in range(DIL_GROUPS)],
        "w_uq": _take(p["mla_w_uq"][l], _UQ_RUNS, 1),
        "w_ukv": _take(p["mla_w_ukv"][l], _UKV_RUNS, 1),
        "w_out": _take(p["w_out"][l], _OUT_RUNS, 0),
        "na_bias": _na_bias(p["na_rpb"][l]),
        "sw_bias": _band_bias(sw_tab, SW_RADIUS, 1, LANES, LANES + 2 * SW_RADIUS),
        "sinks": p["sw_sinks"][l][jnp.asarray(SW_PERM)],
        "dil_tab": dil_tab,
    }
    if l % 2 == 0:
        out["wg"] = p["ffn_w_gate"][l // 2].astype(BF16)
        out["wu"] = p["ffn_w_up"][l // 2].astype(BF16)
        out["wd"] = p["ffn_w_down"][l // 2].astype(BF16)
    else:
        out["wr"] = jnp.pad(p["moe_w_router"][l // 2], ((0, 0), (0, LANES - N_EXPERTS)))
        out["wg"] = p["moe_w_gate"][l // 2].astype(BF16)
        out["wu"] = p["moe_w_up"][l // 2].astype(BF16)
        out["wd"] = p["moe_w_down"][l // 2].astype(BF16)
    return out


def _token_mix(proj, hn, lw, p, l, B, S, rope):
    o_a = _na_attn(proj, lw["na_bias"], B, S)
    qm, km, vm = _mla_up(proj, rope[0], rope[1], p["mla_g_q"][l], p["mla_g_kv"][l], lw["w_uq"], lw["w_ukv"], S)
    o_b = _mla_attn(qm, km, vm, B, S)
    o_c = _sw_attn(proj, lw["sw_bias"], lw["sinks"], B, S)
    outs = []
    for g, dil in enumerate(DIL_DILATIONS):
        L = S // dil
        bias = _band_bias(lw["dil_tab"][g], DIL_SIDE, dil, LANES, min(L, LANES + 2 * DIL_SIDE))
        outs.append(_dil_attn(_dil_proj(hn, lw["w_dil"][g], B, S, dil), bias, dil))
    o_d = _merge(outs, B, S)
    return o_a, o_b, o_c, o_d


def _run_trunk(x, mods, layer_w, p, B, S):
    T = B * S
    x = x.reshape(T, D_MODEL)
    rope = _rope_tables(S)
    for l in range(DEPTH):
        lw = layer_w[l]
        mod = mods[l]
        proj, hn = _in_proj(x, p["g_mix"][l], mod, lw["w_main"], S)
        o_a, o_b, o_c, o_d = _token_mix(proj, hn, lw, p, l, B, S, rope)
        x = _out_proj(o_a, o_b, o_c, o_d, lw["w_out"], x, mod, S)
        if l % 2 == 0:
            x = _ffn(x, p["g_ffn"][l], mod, lw["wg"], lw["wu"], lw["wd"], S)
        else:
            x = _moe(x, p["g_ffn"][l], mod, lw["wr"], lw["wg"], lw["wu"], lw["wd"], S)
    return _final_norm(x, p["g_final"]).reshape(B, S, D_MODEL)


def kernel(x_prompt, x_sample, c_prompt, c_sample, w_ada, b_ada, g_mix, g_ffn, w_in, mla_g_q, mla_g_kv, mla_w_uq, mla_w_ukv, na_rpb, sw_sinks, t5_table, w_out, ffn_w_gate, ffn_w_up, ffn_w_down, moe_w_router, moe_w_gate, moe_w_up, moe_w_down, g_final):
    p = dict(g_mix=g_mix, g_ffn=g_ffn, w_in=w_in, mla_g_q=mla_g_q, mla_g_kv=mla_g_kv, mla_w_uq=mla_w_uq,
             mla_w_ukv=mla_w_ukv, na_rpb=na_rpb, sw_sinks=sw_sinks, t5_table=t5_table, w_out=w_out,
             ffn_w_gate=ffn_w_gate, ffn_w_up=ffn_w_up, ffn_w_down=ffn_w_down, moe_w_router=moe_w_router,
             moe_w_gate=moe_w_gate, moe_w_up=moe_w_up, moe_w_down=moe_w_down, g_final=g_final)
    Bp, Sp, _ = x_prompt.shape
    Bs, Ss, _ = x_sample.shape
    rows = 16
    c_pad = jnp.concatenate([c_prompt, c_sample, jnp.zeros((rows - Bp - Bs, D_MODEL), F32)], axis=0)
    mod_all = _ada(c_pad, w_ada, b_ada)
    mods_p = [mod_all[l, :Bp].reshape(Bp, 6, 1, D_MODEL) for l in range(DEPTH)]
    mods_s = [mod_all[l, Bp:Bp + Bs].reshape(Bs, 6, 1, D_MODEL) for l in range(DEPTH)]
    layer_w = [_prep_layer(l, p) for l in range(DEPTH)]
    y_prompt = _run_trunk(x_prompt, mods_p, layer_w, p, Bp, Sp)
    y_sample = _run_trunk(x_sample, mods_s, layer_w, p, Bs, Ss)
    return (y_prompt, y_sample)
```

```python
import functools
import math

import numpy as np
import jax
import jax.numpy as jnp
from jax import lax
from jax.experimental import pallas as pl
from jax.experimental.pallas import tpu as pltpu

F32 = jnp.float32
BF16 = jnp.bfloat16

D_MODEL = 2048
DEPTH = 4
HEAD_DIM = 64
GRID_W = 64
NA_HEADS = 8
NA_ROWS = 8
NA_COLS = 16
MLA_HEADS = 8
MLA_Q_RANK = 512
MLA_KV_RANK = 256
MLA_NOPE = 64
MLA_ROPE = 32
MLA_V = 64
ROPE_THETA = 10000.0
SW_HEADS = 8
SW_KV_HEADS = 2
SW_RADIUS = 128
DIL_DILATIONS = (1, 4, 16)
DIL_GROUPS = 3
DIL_HEADS = 8
DIL_SIDE = 64
T5_BUCKETS = 32
T5_MAX_DIST = 1024
N_EXPERTS = 8
D_FF = 5632
NORM_EPS = 1e-6

LANES = 128
NEG = -1e30
VMEM_LIMIT = 56 * 1024 * 1024
MOE_TM = 512

NA_IN = 3 * NA_HEADS * HEAD_DIM
MLA_IN = MLA_Q_RANK + MLA_KV_RANK + MLA_ROPE
SW_IN = (SW_HEADS + 2 * SW_KV_HEADS) * HEAD_DIM
DIL_IN = 3 * DIL_GROUPS * DIL_HEADS * HEAD_DIM
P_NA = 0
P_CQ = 1536
P_CKV = 2048
P_KR = 2304
P_KRR = 2432
P_SWQ = 2560
P_SWK = 3072
P_SWV = 3200
P_MAIN = 3584
GW = DIL_HEADS * HEAD_DIM
SW_PERM = (0, 4, 1, 5, 2, 6, 3, 7)


def _cparams(*sem):
    return pltpu.CompilerParams(dimension_semantics=sem, vmem_limit_bytes=VMEM_LIMIT)


def _nt_dot(a, b):
    return lax.dot_general(a, b, (((1,), (1,)), ((), ())), preferred_element_type=F32)


def _dot(a, b):
    return jnp.dot(a, b, preferred_element_type=F32)


def _rot_half_cols(base):
    half = MLA_ROPE // 2
    src = [base + half + j for j in range(half)] + [base + j for j in range(half)]
    sgn = [-1.0] * half + [1.0] * half
    return src, sgn


def _main_colmap():
    src = np.zeros((P_MAIN,), np.int32)
    mul = np.zeros((P_MAIN,), np.float32)
    qs = HEAD_DIM ** -0.5
    for j in range(NA_IN):
        src[P_NA + j] = j
        mul[P_NA + j] = qs if j < NA_HEADS * HEAD_DIM else 1.0
    b0 = NA_IN
    for j in range(MLA_Q_RANK + MLA_KV_RANK):
        src[P_CQ + j] = b0 + j
        mul[P_CQ + j] = 1.0
    kr0 = b0 + MLA_Q_RANK + MLA_KV_RANK
    rsrc, rsgn = _rot_half_cols(kr0)
    for rep in range(2):
        for j in range(MLA_ROPE):
            src[P_KR + rep * MLA_ROPE + j] = kr0 + j
            mul[P_KR + rep * MLA_ROPE + j] = 1.0
            src[P_KRR + rep * MLA_ROPE + j] = rsrc[j]
            mul[P_KRR + rep * MLA_ROPE + j] = rsgn[j]
    c0 = NA_IN + MLA_IN
    for hh, h in enumerate(SW_PERM):
        for d in range(HEAD_DIM):
            src[P_SWQ + hh * HEAD_DIM + d] = c0 + h * HEAD_DIM + d
            mul[P_SWQ + hh * HEAD_DIM + d] = qs
    for j in range(2 * SW_KV_HEADS * HEAD_DIM):
        src[P_SWK + j] = c0 + SW_HEADS * HEAD_DIM + j
        mul[P_SWK + j] = 1.0
    return src, mul


def _dil_colmap(g):
    d0 = NA_IN + MLA_IN + SW_IN
    src = np.zeros((3 * GW,), np.int32)
    mul = np.ones((3 * GW,), np.float32)
    for t in range(3):
        for j in range(GW):
            src[t * GW + j] = d0 + (t * DIL_GROUPS + g) * GW + j
    mul[:GW] = HEAD_DIM ** -0.5
    return src, mul


def _mla_q_colmap():
    n = 3 * 4 * LANES
    src = np.zeros((n,), np.int32)
    mul = np.zeros((n,), np.float32)
    hw = MLA_NOPE + MLA_ROPE
    for h in range(MLA_HEADS):
        p, a = divmod(h, 2)
        for d in range(MLA_NOPE):
            src[p * LANES + a * MLA_NOPE + d] = h * hw + d
            mul[p * LANES + a * MLA_NOPE + d] = 1.0
        rsrc, rsgn = _rot_half_cols(h * hw + MLA_NOPE)
        for j in range(MLA_ROPE):
            ca = 4 * LANES + p * LANES + a * MLA_ROPE + j
            cb = 8 * LANES + p * LANES + a * MLA_ROPE + j
            src[ca] = h * hw + MLA_NOPE + j
            mul[ca] = 1.0
            src[cb] = rsrc[j]
            mul[cb] = rsgn[j]
    return src, mul


def _mla_kv_colmap():
    n = 2 * 4 * LANES
    src = np.zeros((n,), np.int32)
    hw = MLA_NOPE + MLA_V
    for h in range(MLA_HEADS):
        for d in range(MLA_NOPE):
            src[h * MLA_NOPE + d] = h * hw + d
            src[4 * LANES + h * MLA_V + d] = h * hw + MLA_NOPE + d
    return src, np.ones((n,), np.float32)


def _out_proj_rowmap():
    src = np.arange(4 * 512, dtype=np.int32)
    for hh, h in enumerate(SW_PERM):
        for d in range(HEAD_DIM):
            src[1024 + hh * HEAD_DIM + d] = 1024 + h * HEAD_DIM + d
    return src, np.ones((4 * 512,), np.float32)


def _runs(colmap):
    src, mul = colmap
    runs, i, n = [], 0, len(src)
    while i < n:
        j = i + 1
        if mul[i] == 0.0:
            while j < n and mul[j] == 0.0:
                j += 1
            runs.append((None, j - i, 0.0))
        else:
            while j < n and mul[j] == mul[i] and src[j] == src[j - 1] + 1:
                j += 1
            runs.append((int(src[i]), j - i, float(mul[i])))
        i = j
    return runs


_MAIN_RUNS = _runs(_main_colmap())
_DIL_RUNS = [_runs(_dil_colmap(g)) for g in range(DIL_GROUPS)]
_UQ_RUNS = _runs(_mla_q_colmap())
_UKV_RUNS = _runs(_mla_kv_colmap())
_OUT_RUNS = _runs(_out_proj_rowmap())


def _take(w, runs, axis):
    parts = []
    for start, n, m in runs:
        if start is None:
            shape = list(w.shape)
            shape[axis] = n
            parts.append(jnp.zeros(shape, w.dtype))
        else:
            piece = lax.slice_in_dim(w, start, start + n, axis=axis)
            parts.append(piece if m == 1.0 else piece * m)
    return jnp.concatenate(parts, axis=axis).astype(BF16)


def _one_hot(idx, n):
    return (idx[..., None] == jnp.arange(n, dtype=idx.dtype)).astype(F32)


def _select(one_hot, table, spec):
    return jnp.einsum(spec, one_hot, table, precision=lax.Precision.HIGHEST, preferred_element_type=F32)


def _t5_bucket(rel):
    half = T5_BUCKETS // 2
    max_exact = half // 2
    side = jnp.where(rel > 0, half, 0)
    n = jnp.abs(rel)
    nf = jnp.maximum(n, 1).astype(F32)
    large = max_exact + (jnp.log(nf / max_exact) / math.log(T5_MAX_DIST / max_exact) * (half - max_exact)).astype(jnp.int32)
    large = jnp.minimum(large, half - 1)
    return side + jnp.where(n < max_exact, n, large)


def _pair_rows(b):
    H, V, q, k = b.shape
    return b.reshape(H // 2, 2, V, q, k).transpose(0, 2, 1, 3, 4).reshape(H // 2, V, 2 * q, k)


def _band_bias(table_hb, radius, dilation, tq, kw):
    qa = np.arange(tq)[:, None]
    kc = np.arange(kw)[None, :]
    outs = []
    for v in range(3):
        rel = kc - qa - v * radius
        valid = jnp.asarray(np.abs(rel) <= radius)
        bucket = _t5_bucket(jnp.asarray(rel * dilation, jnp.int32))
        b = _select(_one_hot(bucket, T5_BUCKETS), table_hb, "qkb,hb->hqk")
        outs.append(jnp.where(valid[None], b, NEG))
    return _pair_rows(jnp.stack(outs, axis=1).astype(F32))


def _na_bias(rpb):
    c = np.arange(GRID_W)[:, None]
    kc = np.arange(GRID_W)[None, :]
    col_start = np.clip(c - NA_COLS // 2, 0, GRID_W - NA_COLS)
    valid = (kc >= col_start) & (kc < col_start + NA_COLS)
    cidx = np.clip(kc - c + NA_COLS - 1, 0, 2 * NA_COLS - 2)
    d = np.arange(NA_ROWS)[:, None]
    kr = np.arange(NA_ROWS)[None, :]
    ridx = kr - d + NA_ROWS - 1
    rsel = _one_hot(jnp.asarray(ridx.reshape(-1)), 2 * NA_ROWS - 1)
    csel = _one_hot(jnp.asarray(cidx.reshape(-1)), 2 * NA_COLS - 1)
    b = _select(rsel, _select(csel, rpb, "cj,hij->hic"), "ri,hic->hrc")
    b = b.reshape(NA_HEADS, NA_ROWS, NA_ROWS, GRID_W, GRID_W)
    b = jnp.where(jnp.asarray(valid)[None, None, None], b, NEG)
    b = b.transpose(0, 1, 3, 2, 4)
    return _pair_rows(b.reshape(NA_HEADS, NA_ROWS, GRID_W, NA_ROWS * GRID_W).astype(F32))


def _rope_tables(S):
    inv = jnp.power(jnp.float32(ROPE_THETA), -jnp.arange(0, MLA_ROPE, 2, dtype=F32) / MLA_ROPE)
    ang = jnp.arange(S, dtype=F32)[:, None] * inv[None, :]
    cos, sin = jnp.cos(ang), jnp.sin(ang)
    z = jnp.zeros((S, LANES - 2 * MLA_ROPE), F32)
    cos_l = jnp.concatenate([cos, cos, cos, cos, z], axis=1)
    sin_l = jnp.concatenate([sin, sin, sin, sin, z], axis=1)
    return cos_l, sin_l


def _ada_kernel(c_ref, w_ref, b_ref, o_ref):
    c = c_ref[...]
    cs = (c * jax.nn.sigmoid(c)).astype(BF16)
    o_ref[...] = _dot(cs, w_ref[...].astype(BF16)) + b_ref[...]


def _ada(c_pad, w_ada, b_ada):
    L, D, N = w_ada.shape
    R = c_pad.shape[0]
    tn = 1024
    return pl.pallas_call(
        _ada_kernel,
        grid=(L, N // tn),
        in_specs=[
            pl.BlockSpec((R, D), lambda l, j: (0, 0)),
            pl.BlockSpec((None, D, tn), lambda l, j: (l, 0, j)),
            pl.BlockSpec((None, 1, tn), lambda l, j: (l, 0, j)),
        ],
        out_specs=pl.BlockSpec((None, R, tn), lambda l, j: (l, 0, j)),
        out_shape=jax.ShapeDtypeStruct((L, R, N), F32),
        compiler_params=_cparams("parallel", "parallel"),
        name="ada",
    )(c_pad, w_ada, b_ada.reshape(L, 1, N))


def _norm_mod(x, g, sh, sc):
    y = x * lax.rsqrt(jnp.mean(x * x, axis=-1, keepdims=True) + NORM_EPS) * g
    return y * (1.0 + sc) + sh


def _mod_spec(k, tm, S):
    return pl.BlockSpec((None, None, 1, D_MODEL), lambda i, *_: ((i * tm) // S, k, 0, 0))


def _lane_half_mask(shape):
    return lax.broadcasted_iota(jnp.int32, shape, len(shape) - 1) < HEAD_DIM


def _stack_heads(q, first):
    zero = jnp.zeros_like(q)
    return jnp.concatenate([jnp.where(first, q, zero), jnp.where(first, zero, q)], axis=0)


def _in_kernel(x_ref, g_ref, sh_ref, sc_ref, w_ref, o_ref, hn_ref):
    @pl.when(pl.program_id(1) == 0)
    def _():
        hn_ref[...] = _norm_mod(x_ref[...], g_ref[...], sh_ref[...], sc_ref[...]).astype(BF16)

    o_ref[...] = _dot(hn_ref[...], w_ref[...]).astype(BF16)


def _in_proj(x, g, mod, w, S):
    T, D = x.shape
    N = w.shape[1]
    tm, tn = 512, N // 2
    return pl.pallas_call(
        _in_kernel,
        grid=(T // tm, N // tn),
        in_specs=[
            pl.BlockSpec((tm, D), lambda i, j: (i, 0)),
            pl.BlockSpec((1, D), lambda i, j: (0, 0)),
            _mod_spec(0, tm, S),
            _mod_spec(1, tm, S),
            pl.BlockSpec((D, tn), lambda i, j: (0, j)),
        ],
        out_specs=[pl.BlockSpec((tm, tn), lambda i, j: (i, j)),
                   pl.BlockSpec((tm, D), lambda i, j: (i, 0))],
        out_shape=[jax.ShapeDtypeStruct((T, N), BF16), jax.ShapeDtypeStruct((T, D), BF16)],
        compiler_params=_cparams("parallel", "arbitrary"),
        name="in_proj",
    )(x, g.reshape(1, D), mod, mod, w)


def _dil_proj_kernel(h_ref, w_ref, o_ref, *scratch, dil):
    r = _dot(h_ref[...], w_ref[...])
    if dil == 1:
        o_ref[0] = r.astype(BF16)
    else:
        r_ref, = scratch
        n = r_ref.shape[1] // dil
        for jb in range(r_ref.shape[0]):
            cs = slice(jb * LANES, (jb + 1) * LANES)
            r_ref[jb] = r[:, cs]
            for c in range(dil):
                o_ref[c, :, cs] = r_ref[jb, pl.ds(c, n, stride=dil), :].astype(BF16)


def _dil_proj(hn, w, B, S, dil):
    T, D = hn.shape
    N = w.shape[1]
    tm = 1024
    nsb = S // tm
    scratch = [] if dil == 1 else [pltpu.VMEM((N // LANES, tm, LANES), F32)]
    return pl.pallas_call(
        functools.partial(_dil_proj_kernel, dil=dil),
        grid=(T // tm,),
        in_specs=[pl.BlockSpec((tm, D), lambda i: (i, 0)), pl.BlockSpec((D, N), lambda i: (0, 0))],
        out_specs=pl.BlockSpec((None, dil, tm // dil, N), lambda i: (i // nsb, 0, i % nsb, 0)),
        out_shape=jax.ShapeDtypeStruct((B, dil, S // dil, N), BF16),
        scratch_shapes=scratch,
        compiler_params=_cparams("parallel"),
        name="dil_proj_d%d" % dil,
    )(hn, w)


def _softmax_pv(s, v):
    m = jnp.max(s, axis=-1, keepdims=True)
    p = jnp.exp(s - m)
    l = jnp.sum(p, axis=-1, keepdims=True)
    return _dot(p.astype(BF16), v) / l


def _issue_ahead(n_items, scores, finish):
    cur = scores(0)
    for i in range(n_items):
        nxt = scores(i + 1) if i + 1 < n_items else None
        finish(i, cur)
        cur = nxt


def _na_kernel(q_ref, k_ref, v_ref, bias_ref, o_ref, *, rows, qrows):
    rb = pl.program_id(1)
    kwin = NA_ROWS * GRID_W
    n_pairs = NA_HEADS // 2
    lo = _lane_half_mask((GRID_W, LANES))

    def window(qr):
        r = rb * qrows + qr
        r_start = jnp.clip(r - NA_ROWS // 2, 0, rows - NA_ROWS)
        return r - r_start, pl.multiple_of(r_start * GRID_W, GRID_W)

    def scores(qr):
        d, k0 = window(qr)
        out = []
        for p in range(n_pairs):
            cs = slice(p * LANES, (p + 1) * LANES)
            q = q_ref[qr * GRID_W:(qr + 1) * GRID_W, cs]
            out.append(_nt_dot(_stack_heads(q, lo), k_ref[pl.ds(k0, kwin), cs]) + bias_ref[p, d])
        return out

    def finish(qr, ss):
        _, k0 = window(qr)
        for p in range(n_pairs):
            cs = slice(p * LANES, (p + 1) * LANES)
            o = _softmax_pv(ss[p], v_ref[pl.ds(k0, kwin), cs])
            o_ref[qr * GRID_W:(qr + 1) * GRID_W, cs] = jnp.where(lo, o[:GRID_W], o[GRID_W:]).astype(BF16)

    _issue_ahead(qrows, scores, finish)


def _na_attn(proj, bias, B, S):
    T = B * S
    rows = S // GRID_W
    qrows = 8
    tq = qrows * GRID_W
    nqb = S // tq
    w = NA_HEADS * HEAD_DIM
    kern = functools.partial(_na_kernel, rows=rows, qrows=qrows)
    return pl.pallas_call(
        kern,
        grid=(B, nqb),
        in_specs=[
            pl.BlockSpec((tq, w), lambda b, r: (b * nqb + r, P_NA // w)),
            pl.BlockSpec((S, w), lambda b, r: (b, P_NA // w + 1)),
            pl.BlockSpec((S, w), lambda b, r: (b, P_NA // w + 2)),
            pl.BlockSpec(bias.shape, lambda b, r: (0, 0, 0, 0)),
        ],
        out_specs=pl.BlockSpec((tq, w), lambda b, r: (b * nqb + r, 0)),
        out_shape=jax.ShapeDtypeStruct((T, w), BF16),
        compiler_params=_cparams("parallel", "arbitrary"),
        name="na_attn",
    )(proj, proj, proj, bias)


def _mla_up_kernel(cq_ref, ckv_ref, kr_ref, krr_ref, cos_ref, sin_ref, gq_ref, gkv_ref, wq_ref, wkv_ref,
                   q_ref, k_ref, v_ref):
    scale = (MLA_NOPE + MLA_ROPE) ** -0.5
    cos = cos_ref[...]
    sin = sin_ref[...]

    def rms(x_ref, g_ref):
        x = x_ref[...].astype(F32)
        return (x * lax.rsqrt(jnp.mean(x * x, axis=-1, keepdims=True) + NORM_EPS) * g_ref[...]).astype(BF16)

    qf = _dot(rms(cq_ref, gq_ref), wq_ref[...]) * scale
    kvf = _dot(rms(ckv_ref, gkv_ref), wkv_ref[...])
    kpe = (kr_ref[...].astype(F32) * cos + krr_ref[...].astype(F32) * sin).astype(BF16)
    for p in range(MLA_HEADS // 2):
        nope = qf[:, p * LANES:(p + 1) * LANES]
        pe = qf[:, (4 + p) * LANES:(5 + p) * LANES] * cos + qf[:, (8 + p) * LANES:(9 + p) * LANES] * sin
        q_ref[:, 2 * p * LANES:(2 * p + 1) * LANES] = nope.astype(BF16)
        q_ref[:, (2 * p + 1) * LANES:(2 * p + 2) * LANES] = pe.astype(BF16)
        k_ref[:, 2 * p * LANES:(2 * p + 1) * LANES] = kvf[:, p * LANES:(p + 1) * LANES].astype(BF16)
        k_ref[:, (2 * p + 1) * LANES:(2 * p + 2) * LANES] = kpe
    v_ref[...] = kvf[:, 4 * LANES:].astype(BF16)


def _mla_up(proj, cos_l, sin_l, g_q, g_kv, wq, wkv, S):
    T = proj.shape[0]
    tm = 512
    nsb = S // tm
    row = lambda i: (i, 0)
    return pl.pallas_call(
        _mla_up_kernel,
        grid=(T // tm,),
        in_specs=[
            pl.BlockSpec((tm, MLA_Q_RANK), lambda i: (i, P_CQ // MLA_Q_RANK)),
            pl.BlockSpec((tm, MLA_KV_RANK), lambda i: (i, P_CKV // MLA_KV_RANK)),
            pl.BlockSpec((tm, LANES), lambda i: (i, P_KR // LANES)),
            pl.BlockSpec((tm, LANES), lambda i: (i, P_KRR // LANES)),
            pl.BlockSpec((tm, LANES), lambda i: (i % nsb, 0)),
            pl.BlockSpec((tm, LANES), lambda i: (i % nsb, 0)),
            pl.BlockSpec((1, MLA_Q_RANK), lambda i: (0, 0)),
            pl.BlockSpec((1, MLA_KV_RANK), lambda i: (0, 0)),
            pl.BlockSpec(wq.shape, lambda i: (0, 0)),
            pl.BlockSpec(wkv.shape, lambda i: (0, 0)),
        ],
        out_specs=[
            pl.BlockSpec((tm, 8 * LANES), row),
            pl.BlockSpec((tm, 8 * LANES), row),
            pl.BlockSpec((tm, 4 * LANES), row),
        ],
        out_shape=[
            jax.ShapeDtypeStruct((T, 8 * LANES), BF16),
            jax.ShapeDtypeStruct((T, 8 * LANES), BF16),
            jax.ShapeDtypeStruct((T, 4 * LANES), BF16),
        ],
        compiler_params=_cparams("parallel"),
        name="mla_up",
    )(proj, proj, proj, proj, cos_l, sin_l, g_q.reshape(1, -1), g_kv.reshape(1, -1), wq, wkv)


def _mla_attn_kernel(q_ref, k_ref, v_ref, o_ref, vt_ref, *, tk):
    tq = q_ref.shape[0]
    nk = vt_ref.shape[0]

    @pl.when(pl.program_id(2) == 0)
    def _():
        for c in range(nk):
            vt_ref[c] = v_ref[c * tk:(c + 1) * tk, :].astype(F32).T.astype(BF16)

    q = q_ref[...]
    lane = lax.broadcasted_iota(jnp.int32, q.shape, 1)
    first = (lane < MLA_NOPE) | ((lane >= LANES) & (lane < LANES + MLA_ROPE))
    qcat = _stack_heads(q, first)

    def scores(c):
        return _nt_dot(k_ref[c * tk:(c + 1) * tk, :], qcat)

    m = jnp.full((1, 2 * tq), NEG, F32)
    l = jnp.zeros((1, 2 * tq), F32)
    acc = jnp.zeros((LANES, 2 * tq), F32)
    st = scores(0)
    for c in range(nk):
        st_next = scores(c + 1) if c + 1 < nk else None
        mn = jnp.maximum(m, jnp.max(st, axis=0, keepdims=True))
        al = jnp.exp(m - mn)
        p = jnp.exp(st - mn)
        l = al * l + jnp.sum(p, axis=0, keepdims=True)
        acc = al * acc + _dot(vt_ref[c], p.astype(BF16))
        m, st = mn, st_next
    o = (acc / l).T
    lo = _lane_half_mask((tq, LANES))
    o_ref[...] = jnp.where(lo, o[:tq], o[tq:]).astype(BF16)


def _mla_attn(qm, km, vm, B, S):
    T = B * S
    tq, tk = 256, 512
    nqb = S // tq
    kern = functools.partial(_mla_attn_kernel, tk=tk)
    return pl.pallas_call(
        kern,
        grid=(B, MLA_HEADS // 2, nqb),
        in_specs=[
            pl.BlockSpec((tq, 2 * LANES), lambda b, p, i: (b * nqb + i, p)),
            pl.BlockSpec((S, 2 * LANES), lambda b, p, i: (b, p)),
            pl.BlockSpec((S, LANES), lambda b, p, i: (b, p)),
        ],
        out_specs=pl.BlockSpec((tq, LANES), lambda b, p, i: (b * nqb + i, p)),
        out_shape=jax.ShapeDtypeStruct((T, MLA_HEADS * MLA_V), BF16),
        scratch_shapes=[pltpu.VMEM((S // tk, LANES, tk), BF16)],
        compiler_params=_cparams("parallel", "parallel", "arbitrary"),
        name="mla_attn",
    )(qm, km, vm)


def _band_kernel(*refs, L, radius, tq, kw, n_pairs, shared_kv, with_sink, with_lse):
    if with_sink:
        sink_ref, refs = refs[0], refs[1:]
    q_ref, k_ref, v_ref, bias_ref = refs[:4]
    o_ref = refs[4]
    lse_ref = refs[5] if with_lse else None
    qb = pl.program_id(2)
    nq = q_ref.shape[0] // tq
    lo = _lane_half_mask((tq, LANES))
    second = lax.broadcasted_iota(jnp.int32, (2 * tq, 1), 0) >= tq

    def window(t):
        q0 = (qb * nq + t) * tq
        start = jnp.clip(q0 - radius, 0, L - kw)
        return (q0 - start) // radius, pl.multiple_of(start, min(radius, tq))

    def scores(t):
        var, k0 = window(t)
        out = []
        for p in range(n_pairs):
            cs = slice(p * LANES, (p + 1) * LANES)
            kcs = slice(0, LANES) if shared_kv else cs
            q = q_ref[t * tq:(t + 1) * tq, cs]
            out.append(_nt_dot(_stack_heads(q, lo), k_ref[pl.ds(k0, kw), kcs]) + bias_ref[p, var])
        return out

    def finish(t, ss):
        _, k0 = window(t)
        rs = slice(t * tq, (t + 1) * tq)
        for p in range(n_pairs):
            cs = slice(p * LANES, (p + 1) * LANES)
            kcs = slice(0, LANES) if shared_kv else cs
            s = ss[p]
            m = jnp.max(s, axis=-1, keepdims=True)
            if with_sink:
                sk = jnp.where(second, sink_ref[2 * p + 1], sink_ref[2 * p])
                m = jnp.maximum(m, sk)
            e = jnp.exp(s - m)
            l = jnp.sum(e, axis=-1, keepdims=True)
            if with_sink:
                l = l + jnp.exp(sk - m)
            o = _dot(e.astype(BF16), v_ref[pl.ds(k0, kw), kcs]) / l
            if with_lse:
                lse = jnp.broadcast_to(m + jnp.log(l), (2 * tq, LANES))
                o_ref[rs, cs] = jnp.where(lo, o[:tq], o[tq:])
                lse_ref[rs, cs] = jnp.where(lo, lse[:tq], lse[tq:])
            else:
                o_ref[rs, cs] = jnp.where(lo, o[:tq], o[tq:]).astype(BF16)

    _issue_ahead(nq, scores, finish)


def _sw_attn(proj, bias, sinks, B, S):
    T = proj.shape[0]
    tq, radius = LANES, SW_RADIUS
    kw = tq + 2 * radius
    qblk = 512
    nqb = S // qblk
    wq = SW_HEADS * HEAD_DIM
    kern = functools.partial(_band_kernel, L=S, radius=radius, tq=tq, kw=kw, n_pairs=wq // LANES,
                             shared_kv=True, with_sink=True, with_lse=False)
    return pl.pallas_call(
        kern,
        grid=(B, 1, nqb),
        in_specs=[
            pl.BlockSpec(memory_space=pltpu.SMEM),
            pl.BlockSpec((qblk, wq), lambda b, c, i: (b * nqb + i, P_SWQ // wq)),
            pl.BlockSpec((S, LANES), lambda b, c, i: (b, P_SWK // LANES)),
            pl.BlockSpec((S, LANES), lambda b, c, i: (b, P_SWV // LANES)),
            pl.BlockSpec(bias.shape, lambda b, c, i: (0, 0, 0, 0)),
        ],
        out_specs=pl.BlockSpec((qblk, wq), lambda b, c, i: (b * nqb + i, 0)),
        out_shape=jax.ShapeDtypeStruct((T, wq), BF16),
        compiler_params=_cparams("parallel", "parallel", "arbitrary"),
        name="band_attn_sw",
    )(sinks, proj, proj, proj, bias)


def _dil_attn(qkv, bias, dil):
    B, _, L, _ = qkv.shape
    tq, radius = LANES, DIL_SIDE
    kw = min(L, tq + 2 * radius)
    qblk = min(L, 512)
    nqb = L // qblk
    kern = functools.partial(_band_kernel, L=L, radius=radius, tq=tq, kw=kw, n_pairs=GW // LANES,
                             shared_kv=False, with_sink=False, with_lse=True)
    o_spec = pl.BlockSpec((None, None, qblk, GW), lambda b, c, i: (b, c, i, 0))
    o_shape = jax.ShapeDtypeStruct((B, dil, L, GW), F32)
    return pl.pallas_call(
        kern,
        grid=(B, dil, nqb),
        in_specs=[
            pl.BlockSpec((None, None, qblk, GW), lambda b, c, i: (b, c, i, 0)),
            pl.BlockSpec((None, None, L, GW), lambda b, c, i: (b, c, 0, 1)),
            pl.BlockSpec((None, None, L, GW), lambda b, c, i: (b, c, 0, 2)),
            pl.BlockSpec(bias.shape, lambda b, c, i: (0, 0, 0, 0)),
        ],
        out_specs=[o_spec, o_spec],
        out_shape=[o_shape, o_shape],
        compiler_params=_cparams("parallel", "parallel", "arbitrary"),
        name="band_attn_d%d" % dil,
    )(qkv, qkv, qkv, bias)


def _merge_kernel(o0, l0, o1, l1, o2, l2, o_ref, *scratch, dils):
    for jb in range(GW // LANES):
        cs = slice(jb * LANES, (jb + 1) * LANES)
        vals = []
        for g, (o_in, l_in) in enumerate(((o0, l0), (o1, l1), (o2, l2))):
            d = dils[g]
            if d == 1:
                vals.append((o_in[0, :, cs], l_in[0, :, cs]))
            else:
                so, sl = scratch[2 * (g - 1)], scratch[2 * (g - 1) + 1]
                n = so.shape[1] // d
                for c in range(d):
                    so[jb, pl.ds(c, n, stride=d), :] = o_in[c, :, cs]
                    sl[jb, pl.ds(c, n, stride=d), :] = l_in[c, :, cs]
                vals.append((so[jb], sl[jb]))
        (a, la), (b, lb), (c, lc) = vals
        m = jnp.maximum(jnp.maximum(la, lb), lc)
        wa, wb, wc = jnp.exp(la - m), jnp.exp(lb - m), jnp.exp(lc - m)
        o_ref[:, cs] = ((wa * a + wb * b + wc * c) / (wa + wb + wc)).astype(BF16)


def _merge(outs, B, S):
    T = B * S
    tm = 1024
    nsb = S // tm
    in_specs, args = [], []
    for d, (o, l) in zip(DIL_DILATIONS, outs):
        spec = pl.BlockSpec((None, d, tm // d, GW), lambda i: (i // nsb, 0, i % nsb, 0))
        in_specs += [spec, spec]
        args += [o, l]
    scratch = [pltpu.VMEM((GW // LANES, tm, LANES), F32) for d in DIL_DILATIONS[1:] for _ in range(2)]
    return pl.pallas_call(
        functools.partial(_merge_kernel, dils=DIL_DILATIONS),
        grid=(T // tm,),
        in_specs=in_specs,
        out_specs=pl.BlockSpec((tm, GW), lambda i: (i, 0)),
        out_shape=jax.ShapeDtypeStruct((T, GW), BF16),
        scratch_shapes=scratch,
        compiler_params=_cparams("parallel"),
        name="dil_merge",
    )(*args)


def _out_kernel(oa_ref, ob_ref, oc_ref, od_ref, w_ref, x_ref, gt_ref, o_ref):
    acc = _dot(oa_ref[...], w_ref[0:512, :])
    acc += _dot(ob_ref[...], w_ref[512:1024, :])
    acc += _dot(oc_ref[...], w_ref[1024:1536, :])
    acc += _dot(od_ref[...], w_ref[1536:2048, :])
    o_ref[...] = x_ref[...] + gt_ref[...] * acc


def _out_proj(oa, ob, oc, od, w, x, mod, S):
    T, D = x.shape
    tm = 512
    mix = pl.BlockSpec((tm, 512), lambda i: (i, 0))
    return pl.pallas_call(
        _out_kernel,
        grid=(T // tm,),
        in_specs=[mix, mix, mix, mix,
                  pl.BlockSpec(w.shape, lambda i: (0, 0)),
                  pl.BlockSpec((tm, D), lambda i: (i, 0)),
                  _mod_spec(2, tm, S)],
        out_specs=pl.BlockSpec((tm, D), lambda i: (i, 0)),
        out_shape=jax.ShapeDtypeStruct((T, D), F32),
        compiler_params=_cparams("parallel"),
        name="out_proj",
    )(oa, ob, oc, od, w, x, mod)


def _swiglu_tile(h, wg, wu, wd):
    g = _dot(h, wg)
    u = _dot(h, wu)
    return _dot((g * jax.nn.sigmoid(g) * u).astype(BF16), wd)


def _ffn_kernel(x_ref, g_ref, sh_ref, sc_ref, gt_ref, wg_ref, wu_ref, wd_ref, o_ref, hn_ref, acc_ref):
    j = pl.program_id(1)

    @pl.when(j == 0)
    def _():
        hn_ref[...] = _norm_mod(x_ref[...], g_ref[...], sh_ref[...], sc_ref[...]).astype(BF16)
        acc_ref[...] = jnp.zeros_like(acc_ref)

    acc_ref[...] += _swiglu_tile(hn_ref[...], wg_ref[...], wu_ref[...], wd_ref[...])

    @pl.when(j == pl.num_programs(1) - 1)
    def _():
        o_ref[...] = x_ref[...] + gt_ref[...] * acc_ref[...]


def _ffn(x, g, mod, wg, wu, wd, S):
    T, D = x.shape
    F = wg.shape[1]
    tm, tf = 512, 512
    return pl.pallas_call(
        _ffn_kernel,
        grid=(T // tm, F // tf),
        in_specs=[
            pl.BlockSpec((tm, D), lambda i, j: (i, 0)),
            pl.BlockSpec((1, D), lambda i, j: (0, 0)),
            _mod_spec(3, tm, S),
            _mod_spec(4, tm, S),
            _mod_spec(5, tm, S),
            pl.BlockSpec((D, tf), lambda i, j: (0, j)),
            pl.BlockSpec((D, tf), lambda i, j: (0, j)),
            pl.BlockSpec((tf, D), lambda i, j: (j, 0)),
        ],
        out_specs=pl.BlockSpec((tm, D), lambda i, j: (i, 0)),
        out_shape=jax.ShapeDtypeStruct((T, D), F32),
        scratch_shapes=[pltpu.VMEM((tm, D), BF16), pltpu.VMEM((tm, D), F32)],
        compiler_params=_cparams("parallel", "arbitrary"),
        name="ffn",
    )(x, g.reshape(1, D), mod, mod, mod, wg, wu, wd)


SEL_I1, SEL_I2, SEL_G1, SEL_G2, SEL_R1, SEL_R2 = range(6)


def _lane_pick(x, lane, k):
    return jnp.sum(jnp.where(lane == k, x, 0.0), axis=-1, keepdims=True)


def _router_kernel(x_ref, g_ref, sh_ref, sc_ref, wr_ref, hn_ref, sel_ref, cnt_ref):
    @pl.when(pl.program_id(0) == 0)
    def _():
        cnt_ref[...] = jnp.zeros_like(cnt_ref)

    h = _norm_mod(x_ref[...], g_ref[...], sh_ref[...], sc_ref[...])
    hn_ref[...] = h
    tm = h.shape[0]
    logits = jnp.dot(h, wr_ref[...], preferred_element_type=F32, precision=lax.Precision.HIGHEST)
    lane = lax.broadcasted_iota(jnp.int32, logits.shape, 1)
    lanef = lane.astype(F32)
    logits = jnp.where(lane < N_EXPERTS, logits, NEG)
    m1 = jnp.max(logits, axis=-1, keepdims=True)
    i1 = jnp.min(jnp.where(logits == m1, lanef, float(LANES)), axis=-1, keepdims=True)
    rest = jnp.where(lanef == i1, NEG, logits)
    m2 = jnp.max(rest, axis=-1, keepdims=True)
    i2 = jnp.min(jnp.where(rest == m2, lanef, float(LANES)), axis=-1, keepdims=True)
    e2 = jnp.exp(m2 - m1)
    g1 = 1.0 / (1.0 + e2)
    g2 = e2 / (1.0 + e2)
    oh1 = jnp.where(lanef == i1, 1.0, 0.0)
    oh2 = jnp.where(lanef == i2, 1.0, 0.0)
    earlier = (lax.broadcasted_iota(jnp.int32, (tm, tm), 1) < lax.broadcasted_iota(jnp.int32, (tm, tm), 0))
    tri = jnp.where(earlier, 1.0, 0.0).astype(BF16)
    pre1 = _dot(tri, oh1.astype(BF16))
    pre2 = _dot(tri, oh2.astype(BF16))
    tot1 = jnp.sum(oh1, axis=0, keepdims=True)
    tot2 = jnp.sum(oh2, axis=0, keepdims=True)
    seen = cnt_ref[...]
    r1 = jnp.sum(oh1 * (pre1 + seen), axis=-1, keepdims=True)
    r2 = jnp.sum(oh2 * (pre2 + seen + tot1), axis=-1, keepdims=True)
    cnt_ref[...] = seen + tot1 + tot2
    sel = jnp.zeros(logits.shape, F32)
    for k, val in ((SEL_I1, i1), (SEL_I2, i2), (SEL_G1, g1), (SEL_G2, g2), (SEL_R1, r1), (SEL_R2, r2)):
        sel = jnp.where(lane == k, val, sel)
    sel_ref[...] = sel


def _router(x, g, mod, wr_pad, S):
    T, D = x.shape
    tm = 512
    return pl.pallas_call(
        _router_kernel,
        grid=(T // tm,),
        in_specs=[
            pl.BlockSpec((tm, D), lambda i: (i, 0)),
            pl.BlockSpec((1, D), lambda i: (0, 0)),
            _mod_spec(3, tm, S),
            _mod_spec(4, tm, S),
            pl.BlockSpec((D, LANES), lambda i: (0, 0)),
        ],
        out_specs=[pl.BlockSpec((tm, D), lambda i: (i, 0)),
                   pl.BlockSpec((tm, LANES), lambda i: (i, 0)),
                   pl.BlockSpec((1, LANES), lambda i: (0, 0))],
        out_shape=[jax.ShapeDtypeStruct((T, D), F32),
                   jax.ShapeDtypeStruct((T, LANES), F32),
                   jax.ShapeDtypeStruct((1, LANES), F32)],
        compiler_params=_cparams("arbitrary"),
        name="router",
    )(x, g.reshape(1, D), mod, mod, wr_pad)


def _route_tables(sel, cnt, T, n_tiles):
    i1 = sel[:, SEL_I1].astype(jnp.int32)
    i2 = sel[:, SEL_I2].astype(jnp.int32)
    r1 = sel[:, SEL_R1].astype(jnp.int32)
    r2 = sel[:, SEL_R2].astype(jnp.int32)
    counts = cnt[0, :N_EXPERTS].astype(jnp.int32)
    tiles_e = (counts + MOE_TM - 1) // MOE_TM
    ends = jnp.cumsum(tiles_e)
    offs = (ends - tiles_e) * MOE_TM
    pos1 = offs[i1] + r1
    pos2 = offs[i2] + r2
    tile = jnp.arange(n_tiles, dtype=jnp.int32)
    n_used = ends[-1]
    valid = (tile < n_used).astype(jnp.int32)
    expert_of = jnp.sum((tile[:, None] >= ends[None, :]).astype(jnp.int32), axis=1)
    expert_of = expert_of[jnp.minimum(tile, n_used - 1)]
    tok = jnp.arange(T, dtype=jnp.int32)
    src = jnp.zeros((n_tiles * MOE_TM,), jnp.int32).at[pos1].set(tok).at[pos2].set(tok)
    return pos1, pos2, src, expert_of, valid


def _row_copy(src_hbm, row, dst_ref, r, sem):
    return pltpu.make_async_copy(src_hbm.at[pl.ds(row, 1)], dst_ref.at[pl.ds(r, 1)], sem)


def _gather_rows(idx_ref, src_hbm, dst_ref, sem):
    n = dst_ref.shape[0]
    unroll = 8

    def start(g, c):
        for u in range(unroll):
            r = g * unroll + u
            _row_copy(src_hbm, idx_ref[r], dst_ref, r, sem).start(priority=u % 2)
        return c

    def wait(g, c):
        for u in range(unroll):
            _row_copy(src_hbm, 0, dst_ref, g * unroll + u, sem).wait()
        return c

    lax.fori_loop(0, n // unroll, start, 0)
    lax.fori_loop(0, n // unroll, wait, 0)


def _dispatch_kernel(src_ref, h_hbm, o_ref, sem):
    _gather_rows(src_ref, h_hbm, o_ref, sem)


def _dispatch(hn, src):
    T, D = hn.shape
    P = src.shape[0]
    tm = 256
    return pl.pallas_call(
        _dispatch_kernel,
        grid=(P // tm,),
        in_specs=[pl.BlockSpec((tm,), lambda i: (i,), memory_space=pltpu.SMEM),
                  pl.BlockSpec(memory_space=pl.ANY)],
        out_specs=pl.BlockSpec((tm, D), lambda i: (i, 0)),
        out_shape=jax.ShapeDtypeStruct((P, D), F32),
        scratch_shapes=[pltpu.SemaphoreType.DMA(())],
        compiler_params=_cparams("arbitrary"),
        name="moe_dispatch",
    )(src, hn)


def _moe_ffn_kernel(te_ref, va_ref, xs_ref, wg_ref, wu_ref, wd_ref, o_ref, hb_ref, acc_ref):
    i = pl.program_id(0)
    j = pl.program_id(1)
    last = pl.num_programs(1) - 1
    busy = va_ref[i] == 1

    @pl.when(busy & (j == 0))
    def _():
        hb_ref[...] = xs_ref[...].astype(BF16)
        acc_ref[...] = jnp.zeros_like(acc_ref)

    @pl.when(busy)
    def _():
        acc_ref[...] += _swiglu_tile(hb_ref[...], wg_ref[...], wu_ref[...], wd_ref[...])

    @pl.when(busy & (j == last))
    def _():
        o_ref[...] = acc_ref[...]

    @pl.when(jnp.logical_not(busy) & (j == last))
    def _():
        o_ref[...] = jnp.zeros_like(o_ref)


def _moe_ffn(xs, expert_of, valid, wg, wu, wd):
    P, D = xs.shape
    E, _, F = wg.shape
    tm, tf = MOE_TM, 512
    nj = F // tf

    def wcol(i, j, te, va):
        return jnp.where(va[i] == 1, j, nj - 1)

    grid_spec = pltpu.PrefetchScalarGridSpec(
        num_scalar_prefetch=2,
        grid=(P // tm, nj),
        in_specs=[
            pl.BlockSpec((tm, D), lambda i, j, te, va: (i, 0)),
            pl.BlockSpec((None, D, tf), lambda i, j, te, va: (te[i], 0, wcol(i, j, te, va))),
            pl.BlockSpec((None, D, tf), lambda i, j, te, va: (te[i], 0, wcol(i, j, te, va))),
            pl.BlockSpec((None, tf, D), lambda i, j, te, va: (te[i], wcol(i, j, te, va), 0)),
        ],
        out_specs=pl.BlockSpec((tm, D), lambda i, j, te, va: (i, 0)),
        scratch_shapes=[pltpu.VMEM((tm, D), BF16), pltpu.VMEM((tm, D), F32)],
    )
    return pl.pallas_call(
        _moe_ffn_kernel,
        grid_spec=grid_spec,
        out_shape=jax.ShapeDtypeStruct((P, D), F32),
        compiler_params=_cparams("arbitrary", "arbitrary"),
        name="moe_ffn",
    )(expert_of, valid, xs, wg, wu, wd)


def _combine_kernel(p1_ref, p2_ref, sel_ref, x_ref, gt_ref, ys_hbm, o_ref, y1_ref, y2_ref, sem1, sem2):
    _gather_rows(p1_ref, ys_hbm, y1_ref, sem1)
    _gather_rows(p2_ref, ys_hbm, y2_ref, sem2)
    sel = sel_ref[...]
    lane = lax.broadcasted_iota(jnp.int32, sel.shape, 1)
    g1 = _lane_pick(sel, lane, SEL_G1)
    g2 = _lane_pick(sel, lane, SEL_G2)
    o_ref[...] = x_ref[...] + gt_ref[...] * (g1 * y1_ref[...] + g2 * y2_ref[...])


def _combine(ys, pos1, pos2, sel, x, mod, S):
    T, D = x.shape
    tm = 256
    idx = pl.BlockSpec((tm,), lambda i: (i,), memory_space=pltpu.SMEM)
    return pl.pallas_call(
        _combine_kernel,
        grid=(T // tm,),
        in_specs=[idx, idx,
                  pl.BlockSpec((tm, LANES), lambda i: (i, 0)),
                  pl.BlockSpec((tm, D), lambda i: (i, 0)),
                  _mod_spec(5, tm, S),
                  pl.BlockSpec(memory_space=pl.ANY)],
        out_specs=pl.BlockSpec((tm, D), lambda i: (i, 0)),
        out_shape=jax.ShapeDtypeStruct((T, D), F32),
        scratch_shapes=[pltpu.VMEM((tm, D), F32), pltpu.VMEM((tm, D), F32),
                        pltpu.SemaphoreType.DMA(()), pltpu.SemaphoreType.DMA(())],
        compiler_params=_cparams("arbitrary"),
        name="moe_combine",
    )(pos1, pos2, sel, x, mod, ys)


def _moe(x, g, mod, wr_pad, wg, wu, wd, S):
    T = x.shape[0]
    n_tiles = (2 * T) // MOE_TM + N_EXPERTS
    hn, sel, cnt = _router(x, g, mod, wr_pad, S)
    pos1, pos2, src, expert_of, valid = _route_tables(sel, cnt, T, n_tiles)
    xs = _dispatch(hn, src)
    ys = _moe_ffn(xs, expert_of, valid, wg, wu, wd)
    return _combine(ys, pos1, pos2, sel, x, mod, S)


def _final_kernel(x_ref, g_ref, o_ref):
    x = x_ref[...]
    o_ref[...] = x * lax.rsqrt(jnp.mean(x * x, axis=-1, keepdims=True) + NORM_EPS) * g_ref[...]


def _final_norm(x, g):
    T, D = x.shape
    tm = 1024
    return pl.pallas_call(
        _final_kernel,
        grid=(T // tm,),
        in_specs=[pl.BlockSpec((tm, D), lambda i: (i, 0)), pl.BlockSpec((1, D), lambda i: (0, 0))],
        out_specs=pl.BlockSpec((tm, D), lambda i: (i, 0)),
        out_shape=jax.ShapeDtypeStruct((T, D), F32),
        compiler_params=_cparams("parallel"),
        name="final_norm",
    )(x, g.reshape(1, D))


def _prep_layer(l, p):
    t5 = p["t5_table"]
    sw_tab = jnp.stack([t5[:, h] for h in SW_PERM])
    dil_tab = t5[:, SW_HEADS:].reshape(T5_BUCKETS, DIL_GROUPS, DIL_HEADS).transpose(1, 2, 0)
    w_in = p["w_in"][l]
    out = {
        "w_main": _take(w_in, _MAIN_RUNS, 1),
        "w_dil": [_take(w_in, _DIL_RUNS[g], 1) for g in range(DIL_GROUPS)],
        "w_uq": _take(p["mla_w_uq"][l], _UQ_RUNS, 1),
        "w_ukv": _take(p["mla_w_ukv"][l], _UKV_RUNS, 1),
        "w_out": _take(p["w_out"][l], _OUT_RUNS, 0),
        "na_bias": _na_bias(p["na_rpb"][l]),
        "sw_bias": _band_bias(sw_tab, SW_RADIUS, 1, LANES, LANES + 2 * SW_RADIUS),
        "sinks": jnp.stack([p["sw_sinks"][l][h] for h in SW_PERM]),
        "dil_tab": dil_tab,
    }
    if l % 2 == 0:
        out["wg"] = p["ffn_w_gate"][l // 2].astype(BF16)
        out["wu"] = p["ffn_w_up"][l // 2].astype(BF16)
        out["wd"] = p["ffn_w_down"][l // 2].astype(BF16)
    else:
        out["wr"] = jnp.pad(p["moe_w_router"][l // 2], ((0, 0), (0, LANES - N_EXPERTS)))
        out["wg"] = p["moe_w_gate"][l // 2].astype(BF16)
        out["wu"] = p["moe_w_up"][l // 2].astype(BF16)
        out["wd"] = p["moe_w_down"][l // 2].astype(BF16)
    return out


def _token_mix(proj, hn, lw, p, l, B, S, rope):
    o_a = _na_attn(proj, lw["na_bias"], B, S)
    qm, km, vm = _mla_up(proj, rope[0], rope[1], p["mla_g_q"][l], p["mla_g_kv"][l], lw["w_uq"], lw["w_ukv"], S)
    o_b = _mla_attn(qm, km, vm, B, S)
    o_c = _sw_attn(proj, lw["sw_bias"], lw["sinks"], B, S)
    outs = []
    for g, dil in enumerate(DIL_DILATIONS):
        L = S // dil
        bias = _band_bias(lw["dil_tab"][g], DIL_SIDE, dil, LANES, min(L, LANES + 2 * DIL_SIDE))
        outs.append(_dil_attn(_dil_proj(hn, lw["w_dil"][g], B, S, dil), bias, dil))
    o_d = _merge(outs, B, S)
    return o_a, o_b, o_c, o_d


def _run_trunk(x, mods, layer_w, p, B, S):
    T = B * S
    x = x.reshape(T, D_MODEL)
    rope = _rope_tables(S)
    for l in range(DEPTH):
        lw = layer_w[l]
        mod = mods[l]
        proj, hn = _in_proj(x, p["g_mix"][l], mod, lw["w_main"], S)
        o_a, o_b, o_c, o_d = _token_mix(proj, hn, lw, p, l, B, S, rope)
        x = _out_proj(o_a, o_b, o_c, o_d, lw["w_out"], x, mod, S)
        if l % 2 == 0:
            x = _ffn(x, p["g_ffn"][l], mod, lw["wg"], lw["wu"], lw["wd"], S)
        else:
            x = _moe(x, p["g_ffn"][l], mod, lw["wr"], lw["wg"], lw["wu"], lw["wd"], S)
    return _final_norm(x, p["g_final"]).reshape(B, S, D_MODEL)


def kernel(x_prompt, x_sample, c_prompt, c_sample, w_ada, b_ada, g_mix, g_ffn, w_in, mla_g_q, mla_g_kv, mla_w_uq, mla_w_ukv, na_rpb, sw_sinks, t5_table, w_out, ffn_w_gate, ffn_w_up, ffn_w_down, moe_w_router, moe_w_gate, moe_w_up, moe_w_down, g_final):
    p = dict(g_mix=g_mix, g_ffn=g_ffn, w_in=w_in, mla_g_q=mla_g_q, mla_g_kv=mla_g_kv, mla_w_uq=mla_w_uq,
             mla_w_ukv=mla_w_ukv, na_rpb=na_rpb, sw_sinks=sw_sinks, t5_table=t5_table, w_out=w_out,
             ffn_w_gate=ffn_w_gate, ffn_w_up=ffn_w_up, ffn_w_down=ffn_w_down, moe_w_router=moe_w_router,
             moe_w_gate=moe_w_gate, moe_w_up=moe_w_up, moe_w_down=moe_w_down, g_final=g_final)
    Bp, Sp, _ = x_prompt.shape
    Bs, Ss, _ = x_sample.shape
    rows = 16
    c_pad = jnp.concatenate([c_prompt, c_sample, jnp.zeros((rows - Bp - Bs, D_MODEL), F32)], axis=0)
    mod_all = _ada(c_pad, w_ada, b_ada)
    mods_p = [mod_all[l, :Bp].reshape(Bp, 6, 1, D_MODEL) for l in range(DEPTH)]
    mods_s = [mod_all[l, Bp:Bp + Bs].reshape(Bs, 6, 1, D_MODEL) for l in range(DEPTH)]
    layer_w = [_prep_layer(l, p) for l in range(DEPTH)]
    y_prompt = _run_trunk(x_prompt, mods_p, layer_w, p, Bp, Sp)
    y_sample = _run_trunk(x_sample, mods_s, layer_w, p, Bs, Ss)
    return (y_prompt, y_sample)
```

```python
import functools
import math

import numpy as np
import jax
import jax.numpy as jnp
from jax import lax
from jax.experimental import pallas as pl
from jax.experimental.pallas import tpu as pltpu

F32 = jnp.float32
BF16 = jnp.bfloat16

D_MODEL = 2048
DEPTH = 4
HEAD_DIM = 64
GRID_W = 64
NA_HEADS = 8
NA_ROWS = 8
NA_COLS = 16
MLA_HEADS = 8
MLA_Q_RANK = 512
MLA_KV_RANK = 256
MLA_NOPE = 64
MLA_ROPE = 32
MLA_V = 64
ROPE_THETA = 10000.0
SW_HEADS = 8
SW_KV_HEADS = 2
SW_RADIUS = 128
DIL_DILATIONS = (1, 4, 16)
DIL_GROUPS = 3
DIL_HEADS = 8
DIL_SIDE = 64
T5_BUCKETS = 32
T5_MAX_DIST = 1024
N_EXPERTS = 8
D_FF = 5632
NORM_EPS = 1e-6

LANES = 128
NEG = -1e30
LOG2E = math.log2(math.e)
VMEM_LIMIT = 56 * 1024 * 1024
MOE_TM = 512
MLA_SUM_ROWS = 16

NA_IN = 3 * NA_HEADS * HEAD_DIM
MLA_IN = MLA_Q_RANK + MLA_KV_RANK + MLA_ROPE
SW_IN = (SW_HEADS + 2 * SW_KV_HEADS) * HEAD_DIM
DIL_IN = 3 * DIL_GROUPS * DIL_HEADS * HEAD_DIM
P_NA = 0
P_CQ = 1536
P_CKV = 2048
P_KR = 2304
P_KRR = 2432
P_SWQ = 2560
P_SWK = 3072
P_SWV = 3200
P_MAIN = 3584
GW = DIL_HEADS * HEAD_DIM
SW_PERM = (0, 4, 1, 5, 2, 6, 3, 7)


def _cparams(*sem):
    return pltpu.CompilerParams(dimension_semantics=sem, vmem_limit_bytes=VMEM_LIMIT)


def _nt_dot(a, b):
    return lax.dot_general(a, b, (((1,), (1,)), ((), ())), preferred_element_type=F32)


def _dot(a, b):
    return jnp.dot(a, b, preferred_element_type=F32)


def _rot_half_cols(base):
    half = MLA_ROPE // 2
    src = [base + half + j for j in range(half)] + [base + j for j in range(half)]
    sgn = [-1.0] * half + [1.0] * half
    return src, sgn


def _main_colmap():
    src = np.zeros((P_MAIN,), np.int32)
    mul = np.zeros((P_MAIN,), np.float32)
    qs = HEAD_DIM ** -0.5 * LOG2E
    for j in range(NA_IN):
        src[P_NA + j] = j
        mul[P_NA + j] = qs if j < NA_HEADS * HEAD_DIM else 1.0
    b0 = NA_IN
    for j in range(MLA_Q_RANK + MLA_KV_RANK):
        src[P_CQ + j] = b0 + j
        mul[P_CQ + j] = 1.0
    kr0 = b0 + MLA_Q_RANK + MLA_KV_RANK
    rsrc, rsgn = _rot_half_cols(kr0)
    for rep in range(2):
        for j in range(MLA_ROPE):
            src[P_KR + rep * MLA_ROPE + j] = kr0 + j
            mul[P_KR + rep * MLA_ROPE + j] = 1.0
            src[P_KRR + rep * MLA_ROPE + j] = rsrc[j]
            mul[P_KRR + rep * MLA_ROPE + j] = rsgn[j]
    c0 = NA_IN + MLA_IN
    for hh, h in enumerate(SW_PERM):
        for d in range(HEAD_DIM):
            src[P_SWQ + hh * HEAD_DIM + d] = c0 + h * HEAD_DIM + d
            mul[P_SWQ + hh * HEAD_DIM + d] = qs
    for j in range(2 * SW_KV_HEADS * HEAD_DIM):
        src[P_SWK + j] = c0 + SW_HEADS * HEAD_DIM + j
        mul[P_SWK + j] = 1.0
    return src, mul


def _dil_colmap(g):
    d0 = NA_IN + MLA_IN + SW_IN
    src = np.zeros((3 * GW,), np.int32)
    mul = np.ones((3 * GW,), np.float32)
    for t in range(3):
        for j in range(GW):
            src[t * GW + j] = d0 + (t * DIL_GROUPS + g) * GW + j
    mul[:GW] = HEAD_DIM ** -0.5 * LOG2E
    return src, mul


def _mla_q_colmap():
    n = 3 * 4 * LANES
    src = np.zeros((n,), np.int32)
    mul = np.zeros((n,), np.float32)
    hw = MLA_NOPE + MLA_ROPE
    for h in range(MLA_HEADS):
        p, a = divmod(h, 2)
        for d in range(MLA_NOPE):
            src[p * LANES + a * MLA_NOPE + d] = h * hw + d
            mul[p * LANES + a * MLA_NOPE + d] = 1.0
        rsrc, rsgn = _rot_half_cols(h * hw + MLA_NOPE)
        for j in range(MLA_ROPE):
            ca = 4 * LANES + p * LANES + a * MLA_ROPE + j
            cb = 8 * LANES + p * LANES + a * MLA_ROPE + j
            src[ca] = h * hw + MLA_NOPE + j
            mul[ca] = 1.0
            src[cb] = rsrc[j]
            mul[cb] = rsgn[j]
    return src, mul


def _mla_kv_colmap():
    n = 2 * 4 * LANES
    src = np.zeros((n,), np.int32)
    hw = MLA_NOPE + MLA_V
    for h in range(MLA_HEADS):
        for d in range(MLA_NOPE):
            src[h * MLA_NOPE + d] = h * hw + d
            src[4 * LANES + h * MLA_V + d] = h * hw + MLA_NOPE + d
    return src, np.ones((n,), np.float32)


def _out_proj_rowmap():
    src = np.arange(4 * 512, dtype=np.int32)
    for hh, h in enumerate(SW_PERM):
        for d in range(HEAD_DIM):
            src[1024 + hh * HEAD_DIM + d] = 1024 + h * HEAD_DIM + d
    return src, np.ones((4 * 512,), np.float32)


def _runs(colmap):
    src, mul = colmap
    runs, i, n = [], 0, len(src)
    while i < n:
        j = i + 1
        if mul[i] == 0.0:
            while j < n and mul[j] == 0.0:
                j += 1
            runs.append((None, j - i, 0.0))
        else:
            while j < n and mul[j] == mul[i] and src[j] == src[j - 1] + 1:
                j += 1
            runs.append((int(src[i]), j - i, float(mul[i])))
        i = j
    return runs


_MAIN_RUNS = _runs(_main_colmap())
_DIL_RUNS = [_runs(_dil_colmap(g)) for g in range(DIL_GROUPS)]
_UQ_RUNS = _runs(_mla_q_colmap())
_UKV_RUNS = _runs(_mla_kv_colmap())
_OUT_RUNS = _runs(_out_proj_rowmap())


def _take(w, runs, axis):
    parts = []
    for start, n, m in runs:
        if start is None:
            shape = list(w.shape)
            shape[axis] = n
            parts.append(jnp.zeros(shape, w.dtype))
        else:
            piece = lax.slice_in_dim(w, start, start + n, axis=axis)
            parts.append(piece if m == 1.0 else piece * m)
    return jnp.concatenate(parts, axis=axis).astype(BF16)


def _one_hot(idx, n):
    return (idx[..., None] == jnp.arange(n, dtype=idx.dtype)).astype(F32)


def _select(one_hot, table, spec):
    return jnp.einsum(spec, one_hot, table, precision=lax.Precision.HIGHEST, preferred_element_type=F32)


def _t5_bucket(rel):
    half = T5_BUCKETS // 2
    max_exact = half // 2
    side = jnp.where(rel > 0, half, 0)
    n = jnp.abs(rel)
    nf = jnp.maximum(n, 1).astype(F32)
    large = max_exact + (jnp.log(nf / max_exact) / math.log(T5_MAX_DIST / max_exact) * (half - max_exact)).astype(jnp.int32)
    large = jnp.minimum(large, half - 1)
    return side + jnp.where(n < max_exact, n, large)


def _pair_rows(b):
    H, V, q, k = b.shape
    return b.reshape(H // 2, 2, V, q, k).transpose(0, 2, 1, 3, 4).reshape(H // 2, V, 2 * q, k)


def _band_bias(table_hb, radius, dilation, tq, kw):
    qa = np.arange(tq)[:, None]
    kc = np.arange(kw)[None, :]
    outs = []
    for v in range(3):
        rel = kc - qa - v * radius
        valid = jnp.asarray(np.abs(rel) <= radius)
        bucket = _t5_bucket(jnp.asarray(rel * dilation, jnp.int32))
        b = _select(_one_hot(bucket, T5_BUCKETS), table_hb, "qkb,hb->hqk")
        outs.append(jnp.where(valid[None], b * LOG2E, NEG))
    return _pair_rows(jnp.stack(outs, axis=1).astype(F32))


def _na_bias(rpb):
    c = np.arange(GRID_W)[:, None]
    kc = np.arange(GRID_W)[None, :]
    col_start = np.clip(c - NA_COLS // 2, 0, GRID_W - NA_COLS)
    valid = (kc >= col_start) & (kc < col_start + NA_COLS)
    cidx = np.clip(kc - c + NA_COLS - 1, 0, 2 * NA_COLS - 2)
    d = np.arange(NA_ROWS)[:, None]
    kr = np.arange(NA_ROWS)[None, :]
    ridx = kr - d + NA_ROWS - 1
    rsel = _one_hot(jnp.asarray(ridx.reshape(-1)), 2 * NA_ROWS - 1)
    csel = _one_hot(jnp.asarray(cidx.reshape(-1)), 2 * NA_COLS - 1)
    b = _select(rsel, _select(csel, rpb, "cj,hij->hic"), "ri,hic->hrc")
    b = b.reshape(NA_HEADS, NA_ROWS, NA_ROWS, GRID_W, GRID_W)
    b = jnp.where(jnp.asarray(valid)[None, None, None], b * LOG2E, NEG)
    b = b.transpose(0, 1, 3, 2, 4)
    return _pair_rows(b.reshape(NA_HEADS, NA_ROWS, GRID_W, NA_ROWS * GRID_W).astype(F32))


def _rope_tables(S):
    inv = jnp.power(jnp.float32(ROPE_THETA), -jnp.arange(0, MLA_ROPE, 2, dtype=F32) / MLA_ROPE)
    ang = jnp.arange(S, dtype=F32)[:, None] * inv[None, :]
    cos, sin = jnp.cos(ang), jnp.sin(ang)
    z = jnp.zeros((S, LANES - 2 * MLA_ROPE), F32)
    cos_l = jnp.concatenate([cos, cos, cos, cos, z], axis=1)
    sin_l = jnp.concatenate([sin, sin, sin, sin, z], axis=1)
    return cos_l, sin_l


def _ada_kernel(c_ref, w_ref, b_ref, o_ref):
    c = c_ref[...]
    cs = (c * jax.nn.sigmoid(c)).astype(BF16)
    o_ref[...] = _dot(cs, w_ref[...].astype(BF16)) + b_ref[...]


def _ada(c_pad, w_ada, b_ada):
    L, D, N = w_ada.shape
    R = c_pad.shape[0]
    tn = 1024
    return pl.pallas_call(
        _ada_kernel,
        grid=(L, N // tn),
        in_specs=[
            pl.BlockSpec((R, D), lambda l, j: (0, 0)),
            pl.BlockSpec((None, D, tn), lambda l, j: (l, 0, j)),
            pl.BlockSpec((None, 1, tn), lambda l, j: (l, 0, j)),
        ],
        out_specs=pl.BlockSpec((None, R, tn), lambda l, j: (l, 0, j)),
        out_shape=jax.ShapeDtypeStruct((L, R, N), F32),
        compiler_params=_cparams("parallel", "parallel"),
        name="ada",
    )(c_pad, w_ada, b_ada.reshape(L, 1, N))


def _norm_mod(x, g, sh, sc):
    y = x * lax.rsqrt(jnp.mean(x * x, axis=-1, keepdims=True) + NORM_EPS) * g
    return y * (1.0 + sc) + sh


def _mod_spec(k, tm, S):
    return pl.BlockSpec((None, None, 1, D_MODEL), lambda i, *_: ((i * tm) // S, k, 0, 0))


def _lane_half_mask(shape):
    return lax.broadcasted_iota(jnp.int32, shape, len(shape) - 1) < HEAD_DIM


def _stack_heads(q, first):
    zero = jnp.zeros_like(q)
    return jnp.concatenate([jnp.where(first, q, zero), jnp.where(first, zero, q)], axis=0)


def _in_kernel(x_ref, g_ref, sh_ref, sc_ref, w_ref, o_ref, hn_ref):
    @pl.when(pl.program_id(1) == 0)
    def _():
        hn_ref[...] = _norm_mod(x_ref[...], g_ref[...], sh_ref[...], sc_ref[...]).astype(BF16)

    o_ref[...] = _dot(hn_ref[...], w_ref[...]).astype(BF16)


def _in_proj(x, g, mod, w, S):
    T, D = x.shape
    N = w.shape[1]
    tm, tn = 512, N // 2
    return pl.pallas_call(
        _in_kernel,
        grid=(T // tm, N // tn),
        in_specs=[
            pl.BlockSpec((tm, D), lambda i, j: (i, 0)),
            pl.BlockSpec((1, D), lambda i, j: (0, 0)),
            _mod_spec(0, tm, S),
            _mod_spec(1, tm, S),
            pl.BlockSpec((D, tn), lambda i, j: (0, j)),
        ],
        out_specs=[pl.BlockSpec((tm, tn), lambda i, j: (i, j)),
                   pl.BlockSpec((tm, D), lambda i, j: (i, 0))],
        out_shape=[jax.ShapeDtypeStruct((T, N), BF16), jax.ShapeDtypeStruct((T, D), BF16)],
        compiler_params=_cparams("parallel", "arbitrary"),
        name="in_proj",
    )(x, g.reshape(1, D), mod, mod, w)


def _dil_proj_kernel(h_ref, w_ref, o_ref, *scratch, dil):
    r = _dot(h_ref[...], w_ref[...])
    if dil == 1:
        o_ref[0] = r.astype(BF16)
    else:
        r_ref, = scratch
        n = r_ref.shape[1] // dil
        for jb in range(r_ref.shape[0]):
            cs = slice(jb * LANES, (jb + 1) * LANES)
            r_ref[jb] = r[:, cs]
            for c in range(dil):
                o_ref[c, :, cs] = r_ref[jb, pl.ds(c, n, stride=dil), :].astype(BF16)


def _dil_proj(hn, w, B, S, dil):
    T, D = hn.shape
    N = w.shape[1]
    tm = 1024
    nsb = S // tm
    scratch = [] if dil == 1 else [pltpu.VMEM((N // LANES, tm, LANES), F32)]
    return pl.pallas_call(
        functools.partial(_dil_proj_kernel, dil=dil),
        grid=(T // tm,),
        in_specs=[pl.BlockSpec((tm, D), lambda i: (i, 0)), pl.BlockSpec((D, N), lambda i: (0, 0))],
        out_specs=pl.BlockSpec((None, dil, tm // dil, N), lambda i: (i // nsb, 0, i % nsb, 0)),
        out_shape=jax.ShapeDtypeStruct((B, dil, S // dil, N), BF16),
        scratch_shapes=scratch,
        compiler_params=_cparams("parallel"),
        name="dil_proj_d%d" % dil,
    )(hn, w)


def _softmax_pv(s, v):
    m = jnp.max(s, axis=-1, keepdims=True)
    p = jnp.exp2(s - m)
    l = jnp.sum(p, axis=-1, keepdims=True)
    return _dot(p.astype(BF16), v) / l


def _issue_ahead(n_items, scores, finish):
    cur = scores(0)
    for i in range(n_items):
        nxt = scores(i + 1) if i + 1 < n_items else None
        finish(i, cur)
        cur = nxt


def _na_kernel(q_ref, k_ref, v_ref, bias_ref, o_ref, *, rows, qrows):
    rb = pl.program_id(1)
    kwin = NA_ROWS * GRID_W
    n_pairs = NA_HEADS // 2
    lo = _lane_half_mask((GRID_W, LANES))

    def window(qr):
        r = rb * qrows + qr
        r_start = jnp.clip(r - NA_ROWS // 2, 0, rows - NA_ROWS)
        return r - r_start, pl.multiple_of(r_start * GRID_W, GRID_W)

    def scores(qr):
        d, k0 = window(qr)
        out = []
        for p in range(n_pairs):
            cs = slice(p * LANES, (p + 1) * LANES)
            q = q_ref[qr * GRID_W:(qr + 1) * GRID_W, cs]
            out.append(_nt_dot(_stack_heads(q, lo), k_ref[pl.ds(k0, kwin), cs]) + bias_ref[p, d])
        return out

    def finish(qr, ss):
        _, k0 = window(qr)
        for p in range(n_pairs):
            cs = slice(p * LANES, (p + 1) * LANES)
            o = _softmax_pv(ss[p], v_ref[pl.ds(k0, kwin), cs])
            o_ref[qr * GRID_W:(qr + 1) * GRID_W, cs] = jnp.where(lo, o[:GRID_W], o[GRID_W:]).astype(BF16)

    _issue_ahead(qrows, scores, finish)


def _na_attn(proj, bias, B, S):
    T = B * S
    rows = S // GRID_W
    qrows = 8
    tq = qrows * GRID_W
    nqb = S // tq
    w = NA_HEADS * HEAD_DIM
    kern = functools.partial(_na_kernel, rows=rows, qrows=qrows)
    return pl.pallas_call(
        kern,
        grid=(B, nqb),
        in_specs=[
            pl.BlockSpec((tq, w), lambda b, r: (b * nqb + r, P_NA // w)),
            pl.BlockSpec((S, w), lambda b, r: (b, P_NA // w + 1)),
            pl.BlockSpec((S, w), lambda b, r: (b, P_NA // w + 2)),
            pl.BlockSpec(bias.shape, lambda b, r: (0, 0, 0, 0)),
        ],
        out_specs=pl.BlockSpec((tq, w), lambda b, r: (b * nqb + r, 0)),
        out_shape=jax.ShapeDtypeStruct((T, w), BF16),
        compiler_params=_cparams("parallel", "arbitrary"),
        name="na_attn",
    )(proj, proj, proj, bias)


def _mla_up_kernel(cq_ref, ckv_ref, kr_ref, krr_ref, cos_ref, sin_ref, gq_ref, gkv_ref, wq_ref, wkv_ref,
                   q_ref, k_ref, v_ref):
    scale = (MLA_NOPE + MLA_ROPE) ** -0.5 * LOG2E
    cos = cos_ref[...]
    sin = sin_ref[...]

    def rms(x_ref, g_ref):
        x = x_ref[...].astype(F32)
        return (x * lax.rsqrt(jnp.mean(x * x, axis=-1, keepdims=True) + NORM_EPS) * g_ref[...]).astype(BF16)

    qf = _dot(rms(cq_ref, gq_ref), wq_ref[...]) * scale
    kvf = _dot(rms(ckv_ref, gkv_ref), wkv_ref[...])
    kpe = (kr_ref[...].astype(F32) * cos + krr_ref[...].astype(F32) * sin).astype(BF16)
    for p in range(MLA_HEADS // 2):
        nope = qf[:, p * LANES:(p + 1) * LANES]
        pe = qf[:, (4 + p) * LANES:(5 + p) * LANES] * cos + qf[:, (8 + p) * LANES:(9 + p) * LANES] * sin
        q_ref[:, 2 * p * LANES:(2 * p + 1) * LANES] = nope.astype(BF16)
        q_ref[:, (2 * p + 1) * LANES:(2 * p + 2) * LANES] = pe.astype(BF16)
        k_ref[:, 2 * p * LANES:(2 * p + 1) * LANES] = kvf[:, p * LANES:(p + 1) * LANES].astype(BF16)
        k_ref[:, (2 * p + 1) * LANES:(2 * p + 2) * LANES] = kpe
    v_ref[...] = kvf[:, 4 * LANES:].astype(BF16)


def _mla_up(proj, cos_l, sin_l, g_q, g_kv, wq, wkv, S):
    T = proj.shape[0]
    tm = 512
    nsb = S // tm
    row = lambda i: (i, 0)
    return pl.pallas_call(
        _mla_up_kernel,
        grid=(T // tm,),
        in_specs=[
            pl.BlockSpec((tm, MLA_Q_RANK), lambda i: (i, P_CQ // MLA_Q_RANK)),
            pl.BlockSpec((tm, MLA_KV_RANK), lambda i: (i, P_CKV // MLA_KV_RANK)),
            pl.BlockSpec((tm, LANES), lambda i: (i, P_KR // LANES)),
            pl.BlockSpec((tm, LANES), lambda i: (i, P_KRR // LANES)),
            pl.BlockSpec((tm, LANES), lambda i: (i % nsb, 0)),
            pl.BlockSpec((tm, LANES), lambda i: (i % nsb, 0)),
            pl.BlockSpec((1, MLA_Q_RANK), lambda i: (0, 0)),
            pl.BlockSpec((1, MLA_KV_RANK), lambda i: (0, 0)),
            pl.BlockSpec(wq.shape, lambda i: (0, 0)),
            pl.BlockSpec(wkv.shape, lambda i: (0, 0)),
        ],
        out_specs=[
            pl.BlockSpec((tm, 8 * LANES), row),
            pl.BlockSpec((tm, 8 * LANES), row),
            pl.BlockSpec((tm, 4 * LANES), row),
        ],
        out_shape=[
            jax.ShapeDtypeStruct((T, 8 * LANES), BF16),
            jax.ShapeDtypeStruct((T, 8 * LANES), BF16),
            jax.ShapeDtypeStruct((T, 4 * LANES), BF16),
        ],
        compiler_params=_cparams("parallel"),
        name="mla_up",
    )(proj, proj, proj, proj, cos_l, sin_l, g_q.reshape(1, -1), g_kv.reshape(1, -1), wq, wkv)


def _mla_attn_kernel(q_ref, k_ref, v_ref, o_ref, vt_ref, *, tk):
    tq = q_ref.shape[0]
    nk = vt_ref.shape[0]

    @pl.when(pl.program_id(2) == 0)
    def _():
        for c in range(nk):
            vt_ref[c, :LANES, :] = v_ref[c * tk:(c + 1) * tk, :].astype(F32).T.astype(BF16)
            vt_ref[c, LANES:, :] = jnp.ones((MLA_SUM_ROWS, tk), BF16)

    q = q_ref[...]
    lane = lax.broadcasted_iota(jnp.int32, q.shape, 1)
    first = (lane < MLA_NOPE) | ((lane >= LANES) & (lane < LANES + MLA_ROPE))
    qcat = _stack_heads(q, first)

    def scores(c):
        return _nt_dot(k_ref[c * tk:(c + 1) * tk, :], qcat)

    m = jnp.full((1, 2 * tq), NEG, F32)
    acc = jnp.zeros((LANES + MLA_SUM_ROWS, 2 * tq), F32)
    st = scores(0)
    for c in range(nk):
        st_next = scores(c + 1) if c + 1 < nk else None
        mn = jnp.maximum(m, jnp.max(st, axis=0, keepdims=True))
        p = jnp.exp2(st - mn)
        acc = jnp.exp2(m - mn) * acc + _dot(vt_ref[c], p.astype(BF16))
        m, st = mn, st_next
    o = (acc[:LANES] / acc[LANES:LANES + 1]).T
    lo = _lane_half_mask((tq, LANES))
    o_ref[...] = jnp.where(lo, o[:tq], o[tq:]).astype(BF16)


def _mla_attn(qm, km, vm, B, S):
    T = B * S
    tq, tk = 256, 512
    nqb = S // tq
    kern = functools.partial(_mla_attn_kernel, tk=tk)
    return pl.pallas_call(
        kern,
        grid=(B, MLA_HEADS // 2, nqb),
        in_specs=[
            pl.BlockSpec((tq, 2 * LANES), lambda b, p, i: (b * nqb + i, p)),
            pl.BlockSpec((S, 2 * LANES), lambda b, p, i: (b, p)),
            pl.BlockSpec((S, LANES), lambda b, p, i: (b, p)),
        ],
        out_specs=pl.BlockSpec((tq, LANES), lambda b, p, i: (b * nqb + i, p)),
        out_shape=jax.ShapeDtypeStruct((T, MLA_HEADS * MLA_V), BF16),
        scratch_shapes=[pltpu.VMEM((S // tk, LANES + MLA_SUM_ROWS, tk), BF16)],
        compiler_params=_cparams("parallel", "parallel", "arbitrary"),
        name="mla_attn",
    )(qm, km, vm)


def _band_kernel(*refs, L, radius, tq, kw, n_pairs, shared_kv, with_sink, with_lse):
    if with_sink:
        sink_ref, refs = refs[0], refs[1:]
    q_ref, k_ref, v_ref, bias_ref = refs[:4]
    o_ref = refs[4]
    lse_ref = refs[5] if with_lse else None
    qb = pl.program_id(2)
    nq = q_ref.shape[0] // tq
    lo = _lane_half_mask((tq, LANES))
    second = lax.broadcasted_iota(jnp.int32, (2 * tq, 1), 0) >= tq

    def window(t):
        q0 = (qb * nq + t) * tq
        start = jnp.clip(q0 - radius, 0, L - kw)
        return (q0 - start) // radius, pl.multiple_of(start, min(radius, tq))

    def scores(t):
        var, k0 = window(t)
        out = []
        for p in range(n_pairs):
            cs = slice(p * LANES, (p + 1) * LANES)
            kcs = slice(0, LANES) if shared_kv else cs
            q = q_ref[t * tq:(t + 1) * tq, cs]
            out.append(_nt_dot(_stack_heads(q, lo), k_ref[pl.ds(k0, kw), kcs]) + bias_ref[p, var])
        return out

    def finish(t, ss):
        _, k0 = window(t)
        rs = slice(t * tq, (t + 1) * tq)
        for p in range(n_pairs):
            cs = slice(p * LANES, (p + 1) * LANES)
            kcs = slice(0, LANES) if shared_kv else cs
            s = ss[p]
            m = jnp.max(s, axis=-1, keepdims=True)
            if with_sink:
                sk = jnp.where(second, sink_ref[2 * p + 1], sink_ref[2 * p])
                m = jnp.maximum(m, sk)
            e = jnp.exp2(s - m)
            l = jnp.sum(e, axis=-1, keepdims=True)
            if with_sink:
                l = l + jnp.exp2(sk - m)
            o = _dot(e.astype(BF16), v_ref[pl.ds(k0, kw), kcs]) / l
            if with_lse:
                lse = jnp.broadcast_to(m + jnp.log2(l), (2 * tq, LANES))
                o_ref[rs, cs] = jnp.where(lo, o[:tq], o[tq:])
                lse_ref[rs, cs] = jnp.where(lo, lse[:tq], lse[tq:])
            else:
                o_ref[rs, cs] = jnp.where(lo, o[:tq], o[tq:]).astype(BF16)

    _issue_ahead(nq, scores, finish)


def _sw_attn(proj, bias, sinks, B, S):
    T = proj.shape[0]
    tq, radius = LANES, SW_RADIUS
    kw = tq + 2 * radius
    qblk = 512
    nqb = S // qblk
    wq = SW_HEADS * HEAD_DIM
    kern = functools.partial(_band_kernel, L=S, radius=radius, tq=tq, kw=kw, n_pairs=wq // LANES,
                             shared_kv=True, with_sink=True, with_lse=False)
    return pl.pallas_call(
        kern,
        grid=(B, 1, nqb),
        in_specs=[
            pl.BlockSpec(memory_space=pltpu.SMEM),
            pl.BlockSpec((qblk, wq), lambda b, c, i: (b * nqb + i, P_SWQ // wq)),
            pl.BlockSpec((S, LANES), lambda b, c, i: (b, P_SWK // LANES)),
            pl.BlockSpec((S, LANES), lambda b, c, i: (b, P_SWV // LANES)),
            pl.BlockSpec(bias.shape, lambda b, c, i: (0, 0, 0, 0)),
        ],
        out_specs=pl.BlockSpec((qblk, wq), lambda b, c, i: (b * nqb + i, 0)),
        out_shape=jax.ShapeDtypeStruct((T, wq), BF16),
        compiler_params=_cparams("parallel", "parallel", "arbitrary"),
        name="band_attn_sw",
    )(sinks, proj, proj, proj, bias)


def _dil_attn(qkv, bias, dil):
    B, _, L, _ = qkv.shape
    tq, radius = LANES, DIL_SIDE
    kw = min(L, tq + 2 * radius)
    qblk = min(L, 512)
    nqb = L // qblk
    kern = functools.partial(_band_kernel, L=L, radius=radius, tq=tq, kw=kw, n_pairs=GW // LANES,
                             shared_kv=False, with_sink=False, with_lse=True)
    o_spec = pl.BlockSpec((None, None, qblk, GW), lambda b, c, i: (b, c, i, 0))
    o_shape = jax.ShapeDtypeStruct((B, dil, L, GW), F32)
    return pl.pallas_call(
        kern,
        grid=(B, dil, nqb),
        in_specs=[
            pl.BlockSpec((None, None, qblk, GW), lambda b, c, i: (b, c, i, 0)),
            pl.BlockSpec((None, None, L, GW), lambda b, c, i: (b, c, 0, 1)),
            pl.BlockSpec((None, None, L, GW), lambda b, c, i: (b, c, 0, 2)),
            pl.BlockSpec(bias.shape, lambda b, c, i: (0, 0, 0, 0)),
        ],
        out_specs=[o_spec, o_spec],
        out_shape=[o_shape, o_shape],
        compiler_params=_cparams("parallel", "parallel", "arbitrary"),
        name="band_attn_d%d" % dil,
    )(qkv, qkv, qkv, bias)


def _merge_kernel(o0, l0, o1, l1, o2, l2, o_ref, *scratch, dils):
    for jb in range(GW // LANES):
        cs = slice(jb * LANES, (jb + 1) * LANES)
        vals = []
        for g, (o_in, l_in) in enumerate(((o0, l0), (o1, l1), (o2, l2))):
            d = dils[g]
            if d == 1:
                vals.append((o_in[0, :, cs], l_in[0, :, cs]))
            else:
                so, sl = scratch[2 * (g - 1)], scratch[2 * (g - 1) + 1]
                n = so.shape[1] // d
                for c in range(d):
                    so[jb, pl.ds(c, n, stride=d), :] = o_in[c, :, cs]
                    sl[jb, pl.ds(c, n, stride=d), :] = l_in[c, :, cs]
                vals.append((so[jb], sl[jb]))
        (a, la), (b, lb), (c, lc) = vals
        m = jnp.maximum(jnp.maximum(la, lb), lc)
        wa, wb, wc = jnp.exp2(la - m), jnp.exp2(lb - m), jnp.exp2(lc - m)
        o_ref[:, cs] = ((wa * a + wb * b + wc * c) / (wa + wb + wc)).astype(BF16)


def _merge(outs, B, S):
    T = B * S
    tm = 1024
    nsb = S // tm
    in_specs, args = [], []
    for d, (o, l) in zip(DIL_DILATIONS, outs):
        spec = pl.BlockSpec((None, d, tm // d, GW), lambda i: (i // nsb, 0, i % nsb, 0))
        in_specs += [spec, spec]
        args += [o, l]
    scratch = [pltpu.VMEM((GW // LANES, tm, LANES), F32) for d in DIL_DILATIONS[1:] for _ in range(2)]
    return pl.pallas_call(
        functools.partial(_merge_kernel, dils=DIL_DILATIONS),
        grid=(T // tm,),
        in_specs=in_specs,
        out_specs=pl.BlockSpec((tm, GW), lambda i: (i, 0)),
        out_shape=jax.ShapeDtypeStruct((T, GW), BF16),
        scratch_shapes=scratch,
        compiler_params=_cparams("parallel"),
        name="dil_merge",
    )(*args)


def _out_kernel(oa_ref, ob_ref, oc_ref, od_ref, w_ref, x_ref, gt_ref, o_ref):
    acc = _dot(oa_ref[...], w_ref[0:512, :])
    acc += _dot(ob_ref[...], w_ref[512:1024, :])
    acc += _dot(oc_ref[...], w_ref[1024:1536, :])
    acc += _dot(od_ref[...], w_ref[1536:2048, :])
    o_ref[...] = x_ref[...] + gt_ref[...] * acc


def _out_proj(oa, ob, oc, od, w, x, mod, S):
    T, D = x.shape
    tm = 512
    mix = pl.BlockSpec((tm, 512), lambda i: (i, 0))
    return pl.pallas_call(
        _out_kernel,
        grid=(T // tm,),
        in_specs=[mix, mix, mix, mix,
                  pl.BlockSpec(w.shape, lambda i: (0, 0)),
                  pl.BlockSpec((tm, D), lambda i: (i, 0)),
                  _mod_spec(2, tm, S)],
        out_specs=pl.BlockSpec((tm, D), lambda i: (i, 0)),
        out_shape=jax.ShapeDtypeStruct((T, D), F32),
        compiler_params=_cparams("parallel"),
        name="out_proj",
    )(oa, ob, oc, od, w, x, mod)


def _swiglu_tile(h, wg, wu, wd):
    g = _dot(h, wg)
    u = _dot(h, wu)
    return _dot((g * jax.nn.sigmoid(g) * u).astype(BF16), wd)


def _ffn_kernel(x_ref, g_ref, sh_ref, sc_ref, gt_ref, wg_ref, wu_ref, wd_ref, o_ref, hn_ref, acc_ref):
    j = pl.program_id(1)

    @pl.when(j == 0)
    def _():
        hn_ref[...] = _norm_mod(x_ref[...], g_ref[...], sh_ref[...], sc_ref[...]).astype(BF16)
        acc_ref[...] = jnp.zeros_like(acc_ref)

    acc_ref[...] += _swiglu_tile(hn_ref[...], wg_ref[...], wu_ref[...], wd_ref[...])

    @pl.when(j == pl.num_programs(1) - 1)
    def _():
        o_ref[...] = x_ref[...] + gt_ref[...] * acc_ref[...]


def _ffn(x, g, mod, wg, wu, wd, S):
    T, D = x.shape
    F = wg.shape[1]
    tm, tf = 512, 512
    return pl.pallas_call(
        _ffn_kernel,
        grid=(T // tm, F // tf),
        in_specs=[
            pl.BlockSpec((tm, D), lambda i, j: (i, 0)),
            pl.BlockSpec((1, D), lambda i, j: (0, 0)),
            _mod_spec(3, tm, S),
            _mod_spec(4, tm, S),
            _mod_spec(5, tm, S),
            pl.BlockSpec((D, tf), lambda i, j: (0, j)),
            pl.BlockSpec((D, tf), lambda i, j: (0, j)),
            pl.BlockSpec((tf, D), lambda i, j: (j, 0)),
        ],
        out_specs=pl.BlockSpec((tm, D), lambda i, j: (i, 0)),
        out_shape=jax.ShapeDtypeStruct((T, D), F32),
        scratch_shapes=[pltpu.VMEM((tm, D), BF16), pltpu.VMEM((tm, D), F32)],
        compiler_params=_cparams("parallel", "arbitrary"),
        name="ffn",
    )(x, g.reshape(1, D), mod, mod, mod, wg, wu, wd)


SEL_I1, SEL_I2, SEL_G1, SEL_G2, SEL_R1, SEL_R2 = range(6)


def _lane_pick(x, lane, k):
    return jnp.sum(jnp.where(lane == k, x, 0.0), axis=-1, keepdims=True)


def _router_kernel(x_ref, g_ref, sh_ref, sc_ref, wr_ref, hn_ref, sel_ref, cnt_ref):
    @pl.when(pl.program_id(0) == 0)
    def _():
        cnt_ref[...] = jnp.zeros_like(cnt_ref)

    h = _norm_mod(x_ref[...], g_ref[...], sh_ref[...], sc_ref[...])
    hn_ref[...] = h
    tm = h.shape[0]
    logits = jnp.dot(h, wr_ref[...], preferred_element_type=F32, precision=lax.Precision.HIGHEST)
    lane = lax.broadcasted_iota(jnp.int32, logits.shape, 1)
    lanef = lane.astype(F32)
    logits = jnp.where(lane < N_EXPERTS, logits, NEG)
    m1 = jnp.max(logits, axis=-1, keepdims=True)
    i1 = jnp.min(jnp.where(logits == m1, lanef, float(LANES)), axis=-1, keepdims=True)
    rest = jnp.where(lanef == i1, NEG, logits)
    m2 = jnp.max(rest, axis=-1, keepdims=True)
    i2 = jnp.min(jnp.where(rest == m2, lanef, float(LANES)), axis=-1, keepdims=True)
    e2 = jnp.exp(m2 - m1)
    g1 = 1.0 / (1.0 + e2)
    g2 = e2 / (1.0 + e2)
    oh1 = jnp.where(lanef == i1, 1.0, 0.0)
    oh2 = jnp.where(lanef == i2, 1.0, 0.0)
    earlier = (lax.broadcasted_iota(jnp.int32, (tm, tm), 1) < lax.broadcasted_iota(jnp.int32, (tm, tm), 0))
    tri = jnp.where(earlier, 1.0, 0.0).astype(BF16)
    pre1 = _dot(tri, oh1.astype(BF16))
    pre2 = _dot(tri, oh2.astype(BF16))
    tot1 = jnp.sum(oh1, axis=0, keepdims=True)
    tot2 = jnp.sum(oh2, axis=0, keepdims=True)
    seen = cnt_ref[...]
    r1 = jnp.sum(oh1 * (pre1 + seen), axis=-1, keepdims=True)
    r2 = jnp.sum(oh2 * (pre2 + seen + tot1), axis=-1, keepdims=True)
    cnt_ref[...] = seen + tot1 + tot2
    sel = jnp.zeros(logits.shape, F32)
    for k, val in ((SEL_I1, i1), (SEL_I2, i2), (SEL_G1, g1), (SEL_G2, g2), (SEL_R1, r1), (SEL_R2, r2)):
        sel = jnp.where(lane == k, val, sel)
    sel_ref[...] = sel


def _router(x, g, mod, wr_pad, S):
    T, D = x.shape
    tm = 512
    return pl.pallas_call(
        _router_kernel,
        grid=(T // tm,),
        in_specs=[
            pl.BlockSpec((tm, D), lambda i: (i, 0)),
            pl.BlockSpec((1, D), lambda i: (0, 0)),
            _mod_spec(3, tm, S),
            _mod_spec(4, tm, S),
            pl.BlockSpec((D, LANES), lambda i: (0, 0)),
        ],
        out_specs=[pl.BlockSpec((tm, D), lambda i: (i, 0)),
                   pl.BlockSpec((tm, LANES), lambda i: (i, 0)),
                   pl.BlockSpec((1, LANES), lambda i: (0, 0))],
        out_shape=[jax.ShapeDtypeStruct((T, D), F32),
                   jax.ShapeDtypeStruct((T, LANES), F32),
                   jax.ShapeDtypeStruct((1, LANES), F32)],
        compiler_params=_cparams("arbitrary"),
        name="router",
    )(x, g.reshape(1, D), mod, mod, wr_pad)


def _route_tables(sel, cnt, T, n_tiles):
    i1 = sel[:, SEL_I1].astype(jnp.int32)
    i2 = sel[:, SEL_I2].astype(jnp.int32)
    r1 = sel[:, SEL_R1].astype(jnp.int32)
    r2 = sel[:, SEL_R2].astype(jnp.int32)
    counts = cnt[0, :N_EXPERTS].astype(jnp.int32)
    tiles_e = (counts + MOE_TM - 1) // MOE_TM
    ends = jnp.cumsum(tiles_e)
    offs = (ends - tiles_e) * MOE_TM
    pos1 = offs[i1] + r1
    pos2 = offs[i2] + r2
    tile = jnp.arange(n_tiles, dtype=jnp.int32)
    n_used = ends[-1]
    valid = (tile < n_used).astype(jnp.int32)
    expert_of = jnp.sum((tile[:, None] >= ends[None, :]).astype(jnp.int32), axis=1)
    expert_of = expert_of[jnp.minimum(tile, n_used - 1)]
    tok = jnp.arange(T, dtype=jnp.int32)
    src = jnp.zeros((n_tiles * MOE_TM,), jnp.int32).at[pos1].set(tok).at[pos2].set(tok)
    return pos1, pos2, src, expert_of, valid


def _row_copy(src_hbm, row, dst_ref, r, sem):
    return pltpu.make_async_copy(src_hbm.at[pl.ds(row, 1)], dst_ref.at[pl.ds(r, 1)], sem)


def _gather_rows(idx_ref, src_hbm, dst_ref, sem):
    n = dst_ref.shape[0]
    unroll = 8

    def start(g, c):
        for u in range(unroll):
            r = g * unroll + u
            _row_copy(src_hbm, idx_ref[r], dst_ref, r, sem).start(priority=u % 2)
        return c

    def wait(g, c):
        for u in range(unroll):
            _row_copy(src_hbm, 0, dst_ref, g * unroll + u, sem).wait()
        return c

    lax.fori_loop(0, n // unroll, start, 0)
    lax.fori_loop(0, n // unroll, wait, 0)


def _dispatch_kernel(src_ref, h_hbm, o_ref, buf_ref, sem):
    _gather_rows(src_ref, h_hbm, buf_ref, sem)
    o_ref[...] = buf_ref[...].astype(BF16)


def _dispatch(hn, src):
    T, D = hn.shape
    P = src.shape[0]
    tm = 256
    return pl.pallas_call(
        _dispatch_kernel,
        grid=(P // tm,),
        in_specs=[pl.BlockSpec((tm,), lambda i: (i,), memory_space=pltpu.SMEM),
                  pl.BlockSpec(memory_space=pl.ANY)],
        out_specs=pl.BlockSpec((tm, D), lambda i: (i, 0)),
        out_shape=jax.ShapeDtypeStruct((P, D), BF16),
        scratch_shapes=[pltpu.VMEM((tm, D), F32), pltpu.SemaphoreType.DMA(())],
        compiler_params=_cparams("arbitrary"),
        name="moe_dispatch",
    )(src, hn)


def _moe_ffn_kernel(te_ref, va_ref, xs_ref, wg_ref, wu_ref, wd_ref, o_ref):
    i = pl.program_id(0)
    j = pl.program_id(1)

    @pl.when(j == 0)
    def _():
        o_ref[...] = jnp.zeros_like(o_ref)

    @pl.when(va_ref[i] == 1)
    def _():
        o_ref[...] += _swiglu_tile(xs_ref[...], wg_ref[...], wu_ref[...], wd_ref[...])


def _moe_ffn(xs, expert_of, valid, wg, wu, wd):
    P, D = xs.shape
    E, _, F = wg.shape
    tm, tf = MOE_TM, 512
    nj = F // tf

    def wcol(i, j, te, va):
        return jnp.where(va[i] == 1, j, nj - 1)

    grid_spec = pltpu.PrefetchScalarGridSpec(
        num_scalar_prefetch=2,
        grid=(P // tm, nj),
        in_specs=[
            pl.BlockSpec((tm, D), lambda i, j, te, va: (i, 0)),
            pl.BlockSpec((None, D, tf), lambda i, j, te, va: (te[i], 0, wcol(i, j, te, va))),
            pl.BlockSpec((None, D, tf), lambda i, j, te, va: (te[i], 0, wcol(i, j, te, va))),
            pl.BlockSpec((None, tf, D), lambda i, j, te, va: (te[i], wcol(i, j, te, va), 0)),
        ],
        out_specs=pl.BlockSpec((tm, D), lambda i, j, te, va: (i, 0)),
    )
    return pl.pallas_call(
        _moe_ffn_kernel,
        grid_spec=grid_spec,
        out_shape=jax.ShapeDtypeStruct((P, D), F32),
        compiler_params=_cparams("arbitrary", "arbitrary"),
        name="moe_ffn",
    )(expert_of, valid, xs, wg, wu, wd)


def _combine_kernel(p1_ref, p2_ref, sel_ref, x_ref, gt_ref, ys_hbm, o_ref, y1_ref, y2_ref, sem1, sem2):
    _gather_rows(p1_ref, ys_hbm, y1_ref, sem1)
    _gather_rows(p2_ref, ys_hbm, y2_ref, sem2)
    sel = sel_ref[...]
    lane = lax.broadcasted_iota(jnp.int32, sel.shape, 1)
    g1 = _lane_pick(sel, lane, SEL_G1)
    g2 = _lane_pick(sel, lane, SEL_G2)
    o_ref[...] = x_ref[...] + gt_ref[...] * (g1 * y1_ref[...] + g2 * y2_ref[...])


def _combine(ys, pos1, pos2, sel, x, mod, S):
    T, D = x.shape
    tm = 256
    idx = pl.BlockSpec((tm,), lambda i: (i,), memory_space=pltpu.SMEM)
    return pl.pallas_call(
        _combine_kernel,
        grid=(T // tm,),
        in_specs=[idx, idx,
                  pl.BlockSpec((tm, LANES), lambda i: (i, 0)),
                  pl.BlockSpec((tm, D), lambda i: (i, 0)),
                  _mod_spec(5, tm, S),
                  pl.BlockSpec(memory_space=pl.ANY)],
        out_specs=pl.BlockSpec((tm, D), lambda i: (i, 0)),
        out_shape=jax.ShapeDtypeStruct((T, D), F32),
        scratch_shapes=[pltpu.VMEM((tm, D), F32), pltpu.VMEM((tm, D), F32),
                        pltpu.SemaphoreType.DMA(()), pltpu.SemaphoreType.DMA(())],
        compiler_params=_cparams("arbitrary"),
        name="moe_combine",
    )(pos1, pos2, sel, x, mod, ys)


def _moe(x, g, mod, wr_pad, wg, wu, wd, S):
    T = x.shape[0]
    n_tiles = (2 * T) // MOE_TM + N_EXPERTS
    hn, sel, cnt = _router(x, g, mod, wr_pad, S)
    pos1, pos2, src, expert_of, valid = _route_tables(sel, cnt, T, n_tiles)
    xs = _dispatch(hn, src)
    ys = _moe_ffn(xs, expert_of, valid, wg, wu, wd)
    return _combine(ys, pos1, pos2, sel, x, mod, S)


def _cast_kernel(w_ref, o_ref):
    o_ref[...] = w_ref[...].astype(BF16)


def _cast_layer(w, l):
    if w.ndim == 3:
        return _cast_layer(w[:, None], l)[0]
    _, E, R, C = w.shape
    tr = 256
    return pl.pallas_call(
        _cast_kernel,
        grid=(E, R // tr),
        in_specs=[pl.BlockSpec((None, None, tr, C), lambda e, r: (l, e, r, 0))],
        out_specs=pl.BlockSpec((None, tr, C), lambda e, r: (e, r, 0)),
        out_shape=jax.ShapeDtypeStruct((E, R, C), BF16),
        compiler_params=_cparams("parallel", "parallel"),
        name="cast_bf16",
    )(w)


def _final_kernel(x_ref, g_ref, o_ref):
    x = x_ref[...]
    o_ref[...] = x * lax.rsqrt(jnp.mean(x * x, axis=-1, keepdims=True) + NORM_EPS) * g_ref[...]


def _final_norm(x, g):
    T, D = x.shape
    tm = 1024
    return pl.pallas_call(
        _final_kernel,
        grid=(T // tm,),
        in_specs=[pl.BlockSpec((tm, D), lambda i: (i, 0)), pl.BlockSpec((1, D), lambda i: (0, 0))],
        out_specs=pl.BlockSpec((tm, D), lambda i: (i, 0)),
        out_shape=jax.ShapeDtypeStruct((T, D), F32),
        compiler_params=_cparams("parallel"),
        name="final_norm",
    )(x, g.reshape(1, D))


def _prep_layer(l, p):
    t5 = p["t5_table"]
    sw_tab = jnp.stack([t5[:, h] for h in SW_PERM])
    dil_tab = t5[:, SW_HEADS:].reshape(T5_BUCKETS, DIL_GROUPS, DIL_HEADS).transpose(1, 2, 0)
    w_in = p["w_in"][l]
    out = {
        "w_main": _take(w_in, _MAIN_RUNS, 1),
        "w_dil": [_take(w_in, _DIL_RUNS[g], 1) for g in range(DIL_GROUPS)],
        "w_uq": _take(p["mla_w_uq"][l], _UQ_RUNS, 1),
        "w_ukv": _take(p["mla_w_ukv"][l], _UKV_RUNS, 1),
        "w_out": _take(p["w_out"][l], _OUT_RUNS, 0),
        "na_bias": _na_bias(p["na_rpb"][l]),
        "sw_bias": _band_bias(sw_tab, SW_RADIUS, 1, LANES, LANES + 2 * SW_RADIUS),
        "sinks": jnp.stack([p["sw_sinks"][l][h] for h in SW_PERM]) * LOG2E,
        "dil_tab": dil_tab,
    }
    if l % 2 == 0:
        out["wg"] = _cast_layer(p["ffn_w_gate"], l // 2)
        out["wu"] = _cast_layer(p["ffn_w_up"], l // 2)
        out["wd"] = _cast_layer(p["ffn_w_down"], l // 2)
    else:
        out["wr"] = jnp.pad(p["moe_w_router"][l // 2], ((0, 0), (0, LANES - N_EXPERTS)))
        out["wg"] = _cast_layer(p["moe_w_gate"], l // 2)
        out["wu"] = _cast_layer(p["moe_w_up"], l // 2)
        out["wd"] = _cast_layer(p["moe_w_down"], l // 2)
    return out


def _token_mix(proj, hn, lw, p, l, B, S, rope):
    o_a = _na_attn(proj, lw["na_bias"], B, S)
    qm, km, vm = _mla_up(proj, rope[0], rope[1], p["mla_g_q"][l], p["mla_g_kv"][l], lw["w_uq"], lw["w_ukv"], S)
    o_b = _mla_attn(qm, km, vm, B, S)
    o_c = _sw_attn(proj, lw["sw_bias"], lw["sinks"], B, S)
    outs = []
    for g, dil in enumerate(DIL_DILATIONS):
        L = S // dil
        bias = _band_bias(lw["dil_tab"][g], DIL_SIDE, dil, LANES, min(L, LANES + 2 * DIL_SIDE))
        outs.append(_dil_attn(_dil_proj(hn, lw["w_dil"][g], B, S, dil), bias, dil))
    o_d = _merge(outs, B, S)
    return o_a, o_b, o_c, o_d


def _run_trunk(x, mods, layer_w, p, B, S):
    T = B * S
    x = x.reshape(T, D_MODEL)
    rope = _rope_tables(S)
    for l in range(DEPTH):
        lw = layer_w[l]
        mod = mods[l]
        proj, hn = _in_proj(x, p["g_mix"][l], mod, lw["w_main"], S)
        o_a, o_b, o_c, o_d = _token_mix(proj, hn, lw, p, l, B, S, rope)
        x = _out_proj(o_a, o_b, o_c, o_d, lw["w_out"], x, mod, S)
        if l % 2 == 0:
            x = _ffn(x, p["g_ffn"][l], mod, lw["wg"], lw["wu"], lw["wd"], S)
        else:
            x = _moe(x, p["g_ffn"][l], mod, lw["wr"], lw["wg"], lw["wu"], lw["wd"], S)
    return _final_norm(x, p["g_final"]).reshape(B, S, D_MODEL)


def kernel(x_prompt, x_sample, c_prompt, c_sample, w_ada, b_ada, g_mix, g_ffn, w_in, mla_g_q, mla_g_kv, mla_w_uq, mla_w_ukv, na_rpb, sw_sinks, t5_table, w_out, ffn_w_gate, ffn_w_up, ffn_w_down, moe_w_router, moe_w_gate, moe_w_up, moe_w_down, g_final):
    p = dict(g_mix=g_mix, g_ffn=g_ffn, w_in=w_in, mla_g_q=mla_g_q, mla_g_kv=mla_g_kv, mla_w_uq=mla_w_uq,
             mla_w_ukv=mla_w_ukv, na_rpb=na_rpb, sw_sinks=sw_sinks, t5_table=t5_table, w_out=w_out,
             ffn_w_gate=ffn_w_gate, ffn_w_up=ffn_w_up, ffn_w_down=ffn_w_down, moe_w_router=moe_w_router,
             moe_w_gate=moe_w_gate, moe_w_up=moe_w_up, moe_w_down=moe_w_down, g_final=g_final)
    Bp, Sp, _ = x_prompt.shape
    Bs, Ss, _ = x_sample.shape
    rows = 16
    c_pad = jnp.concatenate([c_prompt, c_sample, jnp.zeros((rows - Bp - Bs, D_MODEL), F32)], axis=0)
    mod_all = _ada(c_pad, w_ada, b_ada)
    mods_p = [mod_all[l, :Bp].reshape(Bp, 6, 1, D_MODEL) for l in range(DEPTH)]
    mods_s = [mod_all[l, Bp:Bp + Bs].reshape(Bs, 6, 1, D_MODEL) for l in range(DEPTH)]
    layer_w = [_prep_layer(l, p) for l in range(DEPTH)]
    y_prompt = _run_trunk(x_prompt, mods_p, layer_w, p, Bp, Sp)
    y_sample = _run_trunk(x_sample, mods_s, layer_w, p, Bs, Ss)
    return (y_prompt, y_sample)
```

```python
import functools
import math

import numpy as np
import jax
import jax.numpy as jnp
from jax import lax
from jax.experimental import pallas as pl
from jax.experimental.pallas import tpu as pltpu

F32 = jnp.float32
BF16 = jnp.bfloat16

D_MODEL = 2048
DEPTH = 4
HEAD_DIM = 64
GRID_W = 64
NA_HEADS = 8
NA_ROWS = 8
NA_COLS = 16
MLA_HEADS = 8
MLA_Q_RANK = 512
MLA_KV_RANK = 256
MLA_NOPE = 64
MLA_ROPE = 32
MLA_V = 64
ROPE_THETA = 10000.0
SW_HEADS = 8
SW_KV_HEADS = 2
SW_RADIUS = 128
DIL_DILATIONS = (1, 4, 16)
DIL_GROUPS = 3
DIL_HEADS = 8
DIL_SIDE = 64
T5_BUCKETS = 32
T5_MAX_DIST = 1024
N_EXPERTS = 8
D_FF = 5632
NORM_EPS = 1e-6

LANES = 128
NEG = -1e30
LOG2E = math.log2(math.e)
VMEM_LIMIT = 56 * 1024 * 1024
MOE_TM = 512
MLA_SUM_ROWS = 16

NA_IN = 3 * NA_HEADS * HEAD_DIM
MLA_IN = MLA_Q_RANK + MLA_KV_RANK + MLA_ROPE
SW_IN = (SW_HEADS + 2 * SW_KV_HEADS) * HEAD_DIM
DIL_IN = 3 * DIL_GROUPS * DIL_HEADS * HEAD_DIM
P_NA = 0
P_CQ = 1536
P_CKV = 2048
P_KR = 2304
P_KRR = 2432
P_SWQ = 2560
P_SWK = 3072
P_SWV = 3200
P_MAIN = 3584
GW = DIL_HEADS * HEAD_DIM
SW_PERM = (0, 4, 1, 5, 2, 6, 3, 7)


def _cparams(*sem):
    return pltpu.CompilerParams(dimension_semantics=sem, vmem_limit_bytes=VMEM_LIMIT)


def _nt_dot(a, b):
    return lax.dot_general(a, b, (((1,), (1,)), ((), ())), preferred_element_type=F32)


def _dot(a, b):
    return jnp.dot(a, b, preferred_element_type=F32)


def _rot_half_cols(base):
    half = MLA_ROPE // 2
    src = [base + half + j for j in range(half)] + [base + j for j in range(half)]
    sgn = [-1.0] * half + [1.0] * half
    return src, sgn


def _main_colmap():
    src = np.zeros((P_MAIN,), np.int32)
    mul = np.zeros((P_MAIN,), np.float32)
    qs = HEAD_DIM ** -0.5 * LOG2E
    for j in range(NA_IN):
        src[P_NA + j] = j
        mul[P_NA + j] = qs if j < NA_HEADS * HEAD_DIM else 1.0
    b0 = NA_IN
    for j in range(MLA_Q_RANK + MLA_KV_RANK):
        src[P_CQ + j] = b0 + j
        mul[P_CQ + j] = 1.0
    kr0 = b0 + MLA_Q_RANK + MLA_KV_RANK
    rsrc, rsgn = _rot_half_cols(kr0)
    for rep in range(2):
        for j in range(MLA_ROPE):
            src[P_KR + rep * MLA_ROPE + j] = kr0 + j
            mul[P_KR + rep * MLA_ROPE + j] = 1.0
            src[P_KRR + rep * MLA_ROPE + j] = rsrc[j]
            mul[P_KRR + rep * MLA_ROPE + j] = rsgn[j]
    c0 = NA_IN + MLA_IN
    for hh, h in enumerate(SW_PERM):
        for d in range(HEAD_DIM):
            src[P_SWQ + hh * HEAD_DIM + d] = c0 + h * HEAD_DIM + d
            mul[P_SWQ + hh * HEAD_DIM + d] = qs
    for j in range(2 * SW_KV_HEADS * HEAD_DIM):
        src[P_SWK + j] = c0 + SW_HEADS * HEAD_DIM + j
        mul[P_SWK + j] = 1.0
    return src, mul


def _dil_colmap(g):
    d0 = NA_IN + MLA_IN + SW_IN
    src = np.zeros((3 * GW,), np.int32)
    mul = np.ones((3 * GW,), np.float32)
    for t in range(3):
        for j in range(GW):
            src[t * GW + j] = d0 + (t * DIL_GROUPS + g) * GW + j
    mul[:GW] = HEAD_DIM ** -0.5 * LOG2E
    return src, mul


def _mla_q_colmap():
    n = 3 * 4 * LANES
    src = np.zeros((n,), np.int32)
    mul = np.zeros((n,), np.float32)
    hw = MLA_NOPE + MLA_ROPE
    for h in range(MLA_HEADS):
        p, a = divmod(h, 2)
        for d in range(MLA_NOPE):
            src[p * LANES + a * MLA_NOPE + d] = h * hw + d
            mul[p * LANES + a * MLA_NOPE + d] = 1.0
        rsrc, rsgn = _rot_half_cols(h * hw + MLA_NOPE)
        for j in range(MLA_ROPE):
            ca = 4 * LANES + p * LANES + a * MLA_ROPE + j
            cb = 8 * LANES + p * LANES + a * MLA_ROPE + j
            src[ca] = h * hw + MLA_NOPE + j
            mul[ca] = 1.0
            src[cb] = rsrc[j]
            mul[cb] = rsgn[j]
    return src, mul


def _mla_kv_colmap():
    n = 2 * 4 * LANES
    src = np.zeros((n,), np.int32)
    hw = MLA_NOPE + MLA_V
    for h in range(MLA_HEADS):
        for d in range(MLA_NOPE):
            src[h * MLA_NOPE + d] = h * hw + d
            src[4 * LANES + h * MLA_V + d] = h * hw + MLA_NOPE + d
    return src, np.ones((n,), np.float32)


def _out_proj_rowmap():
    src = np.arange(4 * 512, dtype=np.int32)
    for hh, h in enumerate(SW_PERM):
        for d in range(HEAD_DIM):
            src[1024 + hh * HEAD_DIM + d] = 1024 + h * HEAD_DIM + d
    return src, np.ones((4 * 512,), np.float32)


def _runs(colmap):
    src, mul = colmap
    runs, i, n = [], 0, len(src)
    while i < n:
        j = i + 1
        if mul[i] == 0.0:
            while j < n and mul[j] == 0.0:
                j += 1
            runs.append((None, j - i, 0.0))
        else:
            while j < n and mul[j] == mul[i] and src[j] == src[j - 1] + 1:
                j += 1
            runs.append((int(src[i]), j - i, float(mul[i])))
        i = j
    return runs


_MAIN_RUNS = _runs(_main_colmap())
_DIL_RUNS = [_runs(_dil_colmap(g)) for g in range(DIL_GROUPS)]
_UQ_RUNS = _runs(_mla_q_colmap())
_UKV_RUNS = _runs(_mla_kv_colmap())
_OUT_RUNS = _runs(_out_proj_rowmap())


def _take(w, runs, axis):
    parts = []
    for start, n, m in runs:
        if start is None:
            shape = list(w.shape)
            shape[axis] = n
            parts.append(jnp.zeros(shape, w.dtype))
        else:
            piece = lax.slice_in_dim(w, start, start + n, axis=axis)
            parts.append(piece if m == 1.0 else piece * m)
    return jnp.concatenate(parts, axis=axis).astype(BF16)


def _one_hot(idx, n):
    return (idx[..., None] == jnp.arange(n, dtype=idx.dtype)).astype(F32)


def _select(one_hot, table, spec):
    return jnp.einsum(spec, one_hot, table, precision=lax.Precision.HIGHEST, preferred_element_type=F32)


def _t5_bucket(rel):
    half = T5_BUCKETS // 2
    max_exact = half // 2
    side = jnp.where(rel > 0, half, 0)
    n = jnp.abs(rel)
    nf = jnp.maximum(n, 1).astype(F32)
    large = max_exact + (jnp.log(nf / max_exact) / math.log(T5_MAX_DIST / max_exact) * (half - max_exact)).astype(jnp.int32)
    large = jnp.minimum(large, half - 1)
    return side + jnp.where(n < max_exact, n, large)


def _pair_rows(b):
    H, V, q, k = b.shape
    return b.reshape(H // 2, 2, V, q, k).transpose(0, 2, 1, 3, 4).reshape(H // 2, V, 2 * q, k)


def _band_bias(table_hb, radius, dilation, tq, kw):
    qa = np.arange(tq)[:, None]
    kc = np.arange(kw)[None, :]
    outs = []
    for v in range(3):
        rel = kc - qa - v * radius
        valid = jnp.asarray(np.abs(rel) <= radius)
        bucket = _t5_bucket(jnp.asarray(rel * dilation, jnp.int32))
        b = _select(_one_hot(bucket, T5_BUCKETS), table_hb, "qkb,hb->hqk")
        outs.append(jnp.where(valid[None], b * LOG2E, NEG))
    return _pair_rows(jnp.stack(outs, axis=1).astype(F32))


def _na_bias(rpb):
    c = np.arange(GRID_W)[:, None]
    kc = np.arange(GRID_W)[None, :]
    col_start = np.clip(c - NA_COLS // 2, 0, GRID_W - NA_COLS)
    valid = (kc >= col_start) & (kc < col_start + NA_COLS)
    cidx = np.clip(kc - c + NA_COLS - 1, 0, 2 * NA_COLS - 2)
    d = np.arange(NA_ROWS)[:, None]
    kr = np.arange(NA_ROWS)[None, :]
    ridx = kr - d + NA_ROWS - 1
    rsel = _one_hot(jnp.asarray(ridx.reshape(-1)), 2 * NA_ROWS - 1)
    csel = _one_hot(jnp.asarray(cidx.reshape(-1)), 2 * NA_COLS - 1)
    b = _select(rsel, _select(csel, rpb, "cj,hij->hic"), "ri,hic->hrc")
    b = b.reshape(NA_HEADS, NA_ROWS, NA_ROWS, GRID_W, GRID_W)
    b = jnp.where(jnp.asarray(valid)[None, None, None], b * LOG2E, NEG)
    b = b.transpose(0, 1, 3, 2, 4)
    return _pair_rows(b.reshape(NA_HEADS, NA_ROWS, GRID_W, NA_ROWS * GRID_W).astype(F32))


def _rope_tables(S):
    inv = jnp.power(jnp.float32(ROPE_THETA), -jnp.arange(0, MLA_ROPE, 2, dtype=F32) / MLA_ROPE)
    ang = jnp.arange(S, dtype=F32)[:, None] * inv[None, :]
    cos, sin = jnp.cos(ang), jnp.sin(ang)
    z = jnp.zeros((S, LANES - 2 * MLA_ROPE), F32)
    cos_l = jnp.concatenate([cos, cos, cos, cos, z], axis=1)
    sin_l = jnp.concatenate([sin, sin, sin, sin, z], axis=1)
    return cos_l, sin_l


def _ada_kernel(c_ref, w_ref, b_ref, o_ref):
    c = c_ref[...]
    cs = (c * jax.nn.sigmoid(c)).astype(BF16)
    o_ref[...] = _dot(cs, w_ref[...].astype(BF16)) + b_ref[...]


def _ada(c_pad, w_ada, b_ada):
    L, D, N = w_ada.shape
    R = c_pad.shape[0]
    tn = 1024
    return pl.pallas_call(
        _ada_kernel,
        grid=(L, N // tn),
        in_specs=[
            pl.BlockSpec((R, D), lambda l, j: (0, 0)),
            pl.BlockSpec((None, D, tn), lambda l, j: (l, 0, j)),
            pl.BlockSpec((None, 1, tn), lambda l, j: (l, 0, j)),
        ],
        out_specs=pl.BlockSpec((None, R, tn), lambda l, j: (l, 0, j)),
        out_shape=jax.ShapeDtypeStruct((L, R, N), F32),
        compiler_params=_cparams("parallel", "parallel"),
        name="ada",
    )(c_pad, w_ada, b_ada.reshape(L, 1, N))


def _norm_mod(x, g, sh, sc):
    y = x * lax.rsqrt(jnp.mean(x * x, axis=-1, keepdims=True) + NORM_EPS) * g
    return y * (1.0 + sc) + sh


def _mod_spec(k, tm, S):
    return pl.BlockSpec((None, None, 1, D_MODEL), lambda i, *_: ((i * tm) // S, k, 0, 0))


def _lane_half_mask(shape):
    return lax.broadcasted_iota(jnp.int32, shape, len(shape) - 1) < HEAD_DIM


def _stack_heads(q, first):
    zero = jnp.zeros_like(q)
    return jnp.concatenate([jnp.where(first, q, zero), jnp.where(first, zero, q)], axis=0)


def _in_kernel(x_ref, g_ref, sh_ref, sc_ref, w_ref, o_ref, hn_ref):
    @pl.when(pl.program_id(1) == 0)
    def _():
        hn_ref[...] = _norm_mod(x_ref[...], g_ref[...], sh_ref[...], sc_ref[...]).astype(BF16)

    o_ref[...] = _dot(hn_ref[...], w_ref[...]).astype(BF16)


def _in_proj(x, g, mod, w, S):
    T, D = x.shape
    N = w.shape[1]
    tm, tn = 512, N // 2
    return pl.pallas_call(
        _in_kernel,
        grid=(T // tm, N // tn),
        in_specs=[
            pl.BlockSpec((tm, D), lambda i, j: (i, 0)),
            pl.BlockSpec((1, D), lambda i, j: (0, 0)),
            _mod_spec(0, tm, S),
            _mod_spec(1, tm, S),
            pl.BlockSpec((D, tn), lambda i, j: (0, j)),
        ],
        out_specs=[pl.BlockSpec((tm, tn), lambda i, j: (i, j)),
                   pl.BlockSpec((tm, D), lambda i, j: (i, 0))],
        out_shape=[jax.ShapeDtypeStruct((T, N), BF16), jax.ShapeDtypeStruct((T, D), BF16)],
        compiler_params=_cparams("parallel", "arbitrary"),
        name="in_proj",
    )(x, g.reshape(1, D), mod, mod, w)


def _dil_proj_kernel(h_ref, w_ref, o_ref, *scratch, dil):
    r = _dot(h_ref[...], w_ref[...])
    if dil == 1:
        o_ref[0] = r.astype(BF16)
    else:
        r_ref, = scratch
        n = r_ref.shape[1] // dil
        for jb in range(r_ref.shape[0]):
            cs = slice(jb * LANES, (jb + 1) * LANES)
            r_ref[jb] = r[:, cs]
            for c in range(dil):
                o_ref[c, :, cs] = r_ref[jb, pl.ds(c, n, stride=dil), :].astype(BF16)


def _dil_proj(hn, w, B, S, dil):
    T, D = hn.shape
    N = w.shape[1]
    tm = 1024
    nsb = S // tm
    scratch = [] if dil == 1 else [pltpu.VMEM((N // LANES, tm, LANES), F32)]
    return pl.pallas_call(
        functools.partial(_dil_proj_kernel, dil=dil),
        grid=(T // tm,),
        in_specs=[pl.BlockSpec((tm, D), lambda i: (i, 0)), pl.BlockSpec((D, N), lambda i: (0, 0))],
        out_specs=pl.BlockSpec((None, dil, tm // dil, N), lambda i: (i // nsb, 0, i % nsb, 0)),
        out_shape=jax.ShapeDtypeStruct((B, dil, S // dil, N), BF16),
        scratch_shapes=scratch,
        compiler_params=_cparams("parallel"),
        name="dil_proj_d%d" % dil,
    )(hn, w)


def _softmax_pv(s, v):
    m = jnp.max(s, axis=-1, keepdims=True)
    p = jnp.exp2(s - m)
    l = jnp.sum(p, axis=-1, keepdims=True)
    return _dot(p.astype(BF16), v) / l


def _issue_ahead(n_items, scores, finish):
    cur = scores(0)
    for i in range(n_items):
        nxt = scores(i + 1) if i + 1 < n_items else None
        finish(i, cur)
        cur = nxt


def _na_kernel(q_ref, k_ref, v_ref, bias_ref, o_ref, *, rows, qrows):
    rb = pl.program_id(1)
    kwin = NA_ROWS * GRID_W
    n_pairs = NA_HEADS // 2
    lo = _lane_half_mask((GRID_W, LANES))

    def window(qr):
        r = rb * qrows + qr
        r_start = jnp.clip(r - NA_ROWS // 2, 0, rows - NA_ROWS)
        return r - r_start, pl.multiple_of(r_start * GRID_W, GRID_W)

    def scores(qr):
        d, k0 = window(qr)
        out = []
        for p in range(n_pairs):
            cs = slice(p * LANES, (p + 1) * LANES)
            q = q_ref[qr * GRID_W:(qr + 1) * GRID_W, cs]
            out.append(_nt_dot(_stack_heads(q, lo), k_ref[pl.ds(k0, kwin), cs]) + bias_ref[p, d])
        return out

    def finish(qr, ss):
        _, k0 = window(qr)
        for p in range(n_pairs):
            cs = slice(p * LANES, (p + 1) * LANES)
            o = _softmax_pv(ss[p], v_ref[pl.ds(k0, kwin), cs])
            o_ref[qr * GRID_W:(qr + 1) * GRID_W, cs] = jnp.where(lo, o[:GRID_W], o[GRID_W:]).astype(BF16)

    _issue_ahead(qrows, scores, finish)


def _na_attn(proj, bias, B, S):
    T = B * S
    rows = S // GRID_W
    qrows = 8
    tq = qrows * GRID_W
    nqb = S // tq
    w = NA_HEADS * HEAD_DIM
    kern = functools.partial(_na_kernel, rows=rows, qrows=qrows)
    return pl.pallas_call(
        kern,
        grid=(B, nqb),
        in_specs=[
            pl.BlockSpec((tq, w), lambda b, r: (b * nqb + r, P_NA // w)),
            pl.BlockSpec((S, w), lambda b, r: (b, P_NA // w + 1)),
            pl.BlockSpec((S, w), lambda b, r: (b, P_NA // w + 2)),
            pl.BlockSpec(bias.shape, lambda b, r: (0, 0, 0, 0)),
        ],
        out_specs=pl.BlockSpec((tq, w), lambda b, r: (b * nqb + r, 0)),
        out_shape=jax.ShapeDtypeStruct((T, w), BF16),
        compiler_params=_cparams("parallel", "arbitrary"),
        name="na_attn",
    )(proj, proj, proj, bias)


def _mla_up_kernel(cq_ref, ckv_ref, kr_ref, krr_ref, cos_ref, sin_ref, gq_ref, gkv_ref, wq_ref, wkv_ref,
                   q_ref, k_ref, v_ref):
    scale = (MLA_NOPE + MLA_ROPE) ** -0.5 * LOG2E
    cos = cos_ref[...]
    sin = sin_ref[...]

    def rms(x_ref, g_ref):
        x = x_ref[...].astype(F32)
        return (x * lax.rsqrt(jnp.mean(x * x, axis=-1, keepdims=True) + NORM_EPS) * g_ref[...]).astype(BF16)

    qf = _dot(rms(cq_ref, gq_ref), wq_ref[...]) * scale
    kvf = _dot(rms(ckv_ref, gkv_ref), wkv_ref[...])
    kpe = (kr_ref[...].astype(F32) * cos + krr_ref[...].astype(F32) * sin).astype(BF16)
    for p in range(MLA_HEADS // 2):
        nope = qf[:, p * LANES:(p + 1) * LANES]
        pe = qf[:, (4 + p) * LANES:(5 + p) * LANES] * cos + qf[:, (8 + p) * LANES:(9 + p) * LANES] * sin
        q_ref[:, 2 * p * LANES:(2 * p + 1) * LANES] = nope.astype(BF16)
        q_ref[:, (2 * p + 1) * LANES:(2 * p + 2) * LANES] = pe.astype(BF16)
        k_ref[:, 2 * p * LANES:(2 * p + 1) * LANES] = kvf[:, p * LANES:(p + 1) * LANES].astype(BF16)
        k_ref[:, (2 * p + 1) * LANES:(2 * p + 2) * LANES] = kpe
    v_ref[...] = kvf[:, 4 * LANES:].astype(BF16)


def _mla_up(proj, cos_l, sin_l, g_q, g_kv, wq, wkv, S):
    T = proj.shape[0]
    tm = 512
    nsb = S // tm
    row = lambda i: (i, 0)
    return pl.pallas_call(
        _mla_up_kernel,
        grid=(T // tm,),
        in_specs=[
            pl.BlockSpec((tm, MLA_Q_RANK), lambda i: (i, P_CQ // MLA_Q_RANK)),
            pl.BlockSpec((tm, MLA_KV_RANK), lambda i: (i, P_CKV // MLA_KV_RANK)),
            pl.BlockSpec((tm, LANES), lambda i: (i, P_KR // LANES)),
            pl.BlockSpec((tm, LANES), lambda i: (i, P_KRR // LANES)),
            pl.BlockSpec((tm, LANES), lambda i: (i % nsb, 0)),
            pl.BlockSpec((tm, LANES), lambda i: (i % nsb, 0)),
            pl.BlockSpec((1, MLA_Q_RANK), lambda i: (0, 0)),
            pl.BlockSpec((1, MLA_KV_RANK), lambda i: (0, 0)),
            pl.BlockSpec(wq.shape, lambda i: (0, 0)),
            pl.BlockSpec(wkv.shape, lambda i: (0, 0)),
        ],
        out_specs=[
            pl.BlockSpec((tm, 8 * LANES), row),
            pl.BlockSpec((tm, 8 * LANES), row),
            pl.BlockSpec((tm, 4 * LANES), row),
        ],
        out_shape=[
            jax.ShapeDtypeStruct((T, 8 * LANES), BF16),
            jax.ShapeDtypeStruct((T, 8 * LANES), BF16),
            jax.ShapeDtypeStruct((T, 4 * LANES), BF16),
        ],
        compiler_params=_cparams("parallel"),
        name="mla_up",
    )(proj, proj, proj, proj, cos_l, sin_l, g_q.reshape(1, -1), g_kv.reshape(1, -1), wq, wkv)


def _mla_attn_kernel(q_ref, k_ref, v_ref, o_ref, vt_ref, *, tk):
    tq = q_ref.shape[0]
    nk = vt_ref.shape[0]

    @pl.when(pl.program_id(2) == 0)
    def _():
        for c in range(nk):
            vt_ref[c, :LANES, :] = v_ref[c * tk:(c + 1) * tk, :].astype(F32).T.astype(BF16)
            vt_ref[c, LANES:, :] = jnp.ones((MLA_SUM_ROWS, tk), BF16)

    q = q_ref[...]
    lane = lax.broadcasted_iota(jnp.int32, q.shape, 1)
    first = (lane < MLA_NOPE) | ((lane >= LANES) & (lane < LANES + MLA_ROPE))
    qcat = _stack_heads(q, first)

    def scores(c):
        return _nt_dot(k_ref[c * tk:(c + 1) * tk, :], qcat)

    m = jnp.full((1, 2 * tq), NEG, F32)
    acc = jnp.zeros((LANES + MLA_SUM_ROWS, 2 * tq), F32)
    st = scores(0)
    for c in range(nk):
        st_next = scores(c + 1) if c + 1 < nk else None
        mn = jnp.maximum(m, jnp.max(st, axis=0, keepdims=True))
        p = jnp.exp2(st - mn)
        acc = jnp.exp2(m - mn) * acc + _dot(vt_ref[c], p.astype(BF16))
        m, st = mn, st_next
    o = (acc[:LANES] / acc[LANES:LANES + 1]).T
    lo = _lane_half_mask((tq, LANES))
    o_ref[...] = jnp.where(lo, o[:tq], o[tq:]).astype(BF16)


def _mla_attn(qm, km, vm, B, S):
    T = B * S
    tq, tk = 256, 512
    nqb = S // tq
    kern = functools.partial(_mla_attn_kernel, tk=tk)
    return pl.pallas_call(
        kern,
        grid=(B, MLA_HEADS // 2, nqb),
        in_specs=[
            pl.BlockSpec((tq, 2 * LANES), lambda b, p, i: (b * nqb + i, p)),
            pl.BlockSpec((S, 2 * LANES), lambda b, p, i: (b, p)),
            pl.BlockSpec((S, LANES), lambda b, p, i: (b, p)),
        ],
        out_specs=pl.BlockSpec((tq, LANES), lambda b, p, i: (b * nqb + i, p)),
        out_shape=jax.ShapeDtypeStruct((T, MLA_HEADS * MLA_V), BF16),
        scratch_shapes=[pltpu.VMEM((S // tk, LANES + MLA_SUM_ROWS, tk), BF16)],
        compiler_params=_cparams("parallel", "parallel", "arbitrary"),
        name="mla_attn",
    )(qm, km, vm)


def _band_kernel(*refs, L, radius, tq, kw, n_pairs, shared_kv, with_sink, with_lse):
    if with_sink:
        sink_ref, refs = refs[0], refs[1:]
    q_ref, k_ref, v_ref, bias_ref = refs[:4]
    o_ref = refs[4]
    lse_ref = refs[5] if with_lse else None
    qb = pl.program_id(2)
    nq = q_ref.shape[0] // tq
    lo = _lane_half_mask((tq, LANES))
    second = lax.broadcasted_iota(jnp.int32, (2 * tq, 1), 0) >= tq

    def window(t):
        q0 = (qb * nq + t) * tq
        start = jnp.clip(q0 - radius, 0, L - kw)
        return (q0 - start) // radius, pl.multiple_of(start, min(radius, tq))

    def scores(t):
        var, k0 = window(t)
        out = []
        for p in range(n_pairs):
            cs = slice(p * LANES, (p + 1) * LANES)
            kcs = slice(0, LANES) if shared_kv else cs
            q = q_ref[t * tq:(t + 1) * tq, cs]
            out.append(_nt_dot(_stack_heads(q, lo), k_ref[pl.ds(k0, kw), kcs]) + bias_ref[p, var])
        return out

    def finish(t, ss):
        _, k0 = window(t)
        rs = slice(t * tq, (t + 1) * tq)
        for p in range(n_pairs):
            cs = slice(p * LANES, (p + 1) * LANES)
            kcs = slice(0, LANES) if shared_kv else cs
            s = ss[p]
            m = jnp.max(s, axis=-1, keepdims=True)
            if with_sink:
                sk = jnp.where(second, sink_ref[2 * p + 1], sink_ref[2 * p])
                m = jnp.maximum(m, sk)
            e = jnp.exp2(s - m)
            l = jnp.sum(e, axis=-1, keepdims=True)
            if with_sink:
                l = l + jnp.exp2(sk - m)
            o = _dot(e.astype(BF16), v_ref[pl.ds(k0, kw), kcs]) / l
            if with_lse:
                lse = jnp.broadcast_to(m + jnp.log2(l), (2 * tq, LANES))
                o_ref[rs, cs] = jnp.where(lo, o[:tq], o[tq:])
                lse_ref[rs, cs] = jnp.where(lo, lse[:tq], lse[tq:])
            else:
                o_ref[rs, cs] = jnp.where(lo, o[:tq], o[tq:]).astype(BF16)

    _issue_ahead(nq, scores, finish)


def _sw_attn(proj, bias, sinks, B, S):
    T = proj.shape[0]
    tq, radius = LANES, SW_RADIUS
    kw = tq + 2 * radius
    qblk = 512
    nqb = S // qblk
    wq = SW_HEADS * HEAD_DIM
    kern = functools.partial(_band_kernel, L=S, radius=radius, tq=tq, kw=kw, n_pairs=wq // LANES,
                             shared_kv=True, with_sink=True, with_lse=False)
    return pl.pallas_call(
        kern,
        grid=(B, 1, nqb),
        in_specs=[
            pl.BlockSpec(memory_space=pltpu.SMEM),
            pl.BlockSpec((qblk, wq), lambda b, c, i: (b * nqb + i, P_SWQ // wq)),
            pl.BlockSpec((S, LANES), lambda b, c, i: (b, P_SWK // LANES)),
            pl.BlockSpec((S, LANES), lambda b, c, i: (b, P_SWV // LANES)),
            pl.BlockSpec(bias.shape, lambda b, c, i: (0, 0, 0, 0)),
        ],
        out_specs=pl.BlockSpec((qblk, wq), lambda b, c, i: (b * nqb + i, 0)),
        out_shape=jax.ShapeDtypeStruct((T, wq), BF16),
        compiler_params=_cparams("parallel", "parallel", "arbitrary"),
        name="band_attn_sw",
    )(sinks, proj, proj, proj, bias)


def _dil_attn(qkv, bias, dil):
    B, _, L, _ = qkv.shape
    tq, radius = LANES, DIL_SIDE
    kw = min(L, tq + 2 * radius)
    qblk = min(L, 512)
    nqb = L // qblk
    kern = functools.partial(_band_kernel, L=L, radius=radius, tq=tq, kw=kw, n_pairs=GW // LANES,
                             shared_kv=False, with_sink=False, with_lse=True)
    o_spec = pl.BlockSpec((None, None, qblk, GW), lambda b, c, i: (b, c, i, 0))
    o_shape = jax.ShapeDtypeStruct((B, dil, L, GW), F32)
    return pl.pallas_call(
        kern,
        grid=(B, dil, nqb),
        in_specs=[
            pl.BlockSpec((None, None, qblk, GW), lambda b, c, i: (b, c, i, 0)),
            pl.BlockSpec((None, None, L, GW), lambda b, c, i: (b, c, 0, 1)),
            pl.BlockSpec((None, None, L, GW), lambda b, c, i: (b, c, 0, 2)),
            pl.BlockSpec(bias.shape, lambda b, c, i: (0, 0, 0, 0)),
        ],
        out_specs=[o_spec, o_spec],
        out_shape=[o_shape, o_shape],
        compiler_params=_cparams("parallel", "parallel", "arbitrary"),
        name="band_attn_d%d" % dil,
    )(qkv, qkv, qkv, bias)


def _merge_kernel(o0, l0, o1, l1, o2, l2, o_ref, *scratch, dils):
    for jb in range(GW // LANES):
        cs = slice(jb * LANES, (jb + 1) * LANES)
        vals = []
        for g, (o_in, l_in) in enumerate(((o0, l0), (o1, l1), (o2, l2))):
            d = dils[g]
            if d == 1:
                vals.append((o_in[0, :, cs], l_in[0, :, cs]))
            else:
                so, sl = scratch[2 * (g - 1)], scratch[2 * (g - 1) + 1]
                n = so.shape[1] // d
                for c in range(d):
                    so[jb, pl.ds(c, n, stride=d), :] = o_in[c, :, cs]
                    sl[jb, pl.ds(c, n, stride=d), :] = l_in[c, :, cs]
                vals.append((so[jb], sl[jb]))
        (a, la), (b, lb), (c, lc) = vals
        m = jnp.maximum(jnp.maximum(la, lb), lc)
        wa, wb, wc = jnp.exp2(la - m), jnp.exp2(lb - m), jnp.exp2(lc - m)
        o_ref[:, cs] = ((wa * a + wb * b + wc * c) / (wa + wb + wc)).astype(BF16)


def _merge(outs, B, S):
    T = B * S
    tm = 1024
    nsb = S // tm
    in_specs, args = [], []
    for d, (o, l) in zip(DIL_DILATIONS, outs):
        spec = pl.BlockSpec((None, d, tm // d, GW), lambda i: (i // nsb, 0, i % nsb, 0))
        in_specs += [spec, spec]
        args += [o, l]
    scratch = [pltpu.VMEM((GW // LANES, tm, LANES), F32) for d in DIL_DILATIONS[1:] for _ in range(2)]
    return pl.pallas_call(
        functools.partial(_merge_kernel, dils=DIL_DILATIONS),
        grid=(T // tm,),
        in_specs=in_specs,
        out_specs=pl.BlockSpec((tm, GW), lambda i: (i, 0)),
        out_shape=jax.ShapeDtypeStruct((T, GW), BF16),
        scratch_shapes=scratch,
        compiler_params=_cparams("parallel"),
        name="dil_merge",
    )(*args)


def _out_kernel(oa_ref, ob_ref, oc_ref, od_ref, w_ref, x_ref, gt_ref, o_ref):
    acc = _dot(oa_ref[...], w_ref[0:512, :])
    acc += _dot(ob_ref[...], w_ref[512:1024, :])
    acc += _dot(oc_ref[...], w_ref[1024:1536, :])
    acc += _dot(od_ref[...], w_ref[1536:2048, :])
    o_ref[...] = x_ref[...] + gt_ref[...] * acc


def _out_proj(oa, ob, oc, od, w, x, mod, S):
    T, D = x.shape
    tm = 512
    mix = pl.BlockSpec((tm, 512), lambda i: (i, 0))
    return pl.pallas_call(
        _out_kernel,
        grid=(T // tm,),
        in_specs=[mix, mix, mix, mix,
                  pl.BlockSpec(w.shape, lambda i: (0, 0)),
                  pl.BlockSpec((tm, D), lambda i: (i, 0)),
                  _mod_spec(2, tm, S)],
        out_specs=pl.BlockSpec((tm, D), lambda i: (i, 0)),
        out_shape=jax.ShapeDtypeStruct((T, D), F32),
        compiler_params=_cparams("parallel"),
        name="out_proj",
    )(oa, ob, oc, od, w, x, mod)


def _swiglu_tile(h, wg, wu, wd):
    g = _dot(h, wg)
    u = _dot(h, wu)
    return _dot((g * jax.nn.sigmoid(g) * u).astype(BF16), wd)


def _ffn_kernel(x_ref, g_ref, sh_ref, sc_ref, gt_ref, wg_ref, wu_ref, wd_ref, o_ref, hn_ref, acc_ref):
    j = pl.program_id(1)

    @pl.when(j == 0)
    def _():
        hn_ref[...] = _norm_mod(x_ref[...], g_ref[...], sh_ref[...], sc_ref[...]).astype(BF16)
        acc_ref[...] = jnp.zeros_like(acc_ref)

    acc_ref[...] += _swiglu_tile(hn_ref[...], wg_ref[...], wu_ref[...], wd_ref[...])

    @pl.when(j == pl.num_programs(1) - 1)
    def _():
        o_ref[...] = x_ref[...] + gt_ref[...] * acc_ref[...]


def _ffn(x, g, mod, wg, wu, wd, S):
    T, D = x.shape
    F = wg.shape[1]
    tm, tf = 512, 512
    return pl.pallas_call(
        _ffn_kernel,
        grid=(T // tm, F // tf),
        in_specs=[
            pl.BlockSpec((tm, D), lambda i, j: (i, 0)),
            pl.BlockSpec((1, D), lambda i, j: (0, 0)),
            _mod_spec(3, tm, S),
            _mod_spec(4, tm, S),
            _mod_spec(5, tm, S),
            pl.BlockSpec((D, tf), lambda i, j: (0, j)),
            pl.BlockSpec((D, tf), lambda i, j: (0, j)),
            pl.BlockSpec((tf, D), lambda i, j: (j, 0)),
        ],
        out_specs=pl.BlockSpec((tm, D), lambda i, j: (i, 0)),
        out_shape=jax.ShapeDtypeStruct((T, D), F32),
        scratch_shapes=[pltpu.VMEM((tm, D), BF16), pltpu.VMEM((tm, D), F32)],
        compiler_params=_cparams("parallel", "arbitrary"),
        name="ffn",
    )(x, g.reshape(1, D), mod, mod, mod, wg, wu, wd)


SEL_I1, SEL_I2, SEL_G1, SEL_G2, SEL_R1, SEL_R2 = range(6)


def _lane_pick(x, lane, k):
    return jnp.sum(jnp.where(lane == k, x, 0.0), axis=-1, keepdims=True)


def _router_kernel(x_ref, g_ref, sh_ref, sc_ref, wr_ref, hn_ref, sel_ref, cnt_ref):
    @pl.when(pl.program_id(0) == 0)
    def _():
        cnt_ref[...] = jnp.zeros_like(cnt_ref)

    h = _norm_mod(x_ref[...], g_ref[...], sh_ref[...], sc_ref[...])
    hn_ref[...] = h
    tm = h.shape[0]
    logits = jnp.dot(h, wr_ref[...], preferred_element_type=F32, precision=lax.Precision.HIGHEST)
    lane = lax.broadcasted_iota(jnp.int32, logits.shape, 1)
    lanef = lane.astype(F32)
    logits = jnp.where(lane < N_EXPERTS, logits, NEG)
    m1 = jnp.max(logits, axis=-1, keepdims=True)
    i1 = jnp.min(jnp.where(logits == m1, lanef, float(LANES)), axis=-1, keepdims=True)
    rest = jnp.where(lanef == i1, NEG, logits)
    m2 = jnp.max(rest, axis=-1, keepdims=True)
    i2 = jnp.min(jnp.where(rest == m2, lanef, float(LANES)), axis=-1, keepdims=True)
    e2 = jnp.exp(m2 - m1)
    g1 = 1.0 / (1.0 + e2)
    g2 = e2 / (1.0 + e2)
    oh1 = jnp.where(lanef == i1, 1.0, 0.0)
    oh2 = jnp.where(lanef == i2, 1.0, 0.0)
    earlier = (lax.broadcasted_iota(jnp.int32, (tm, tm), 1) < lax.broadcasted_iota(jnp.int32, (tm, tm), 0))
    tri = jnp.where(earlier, 1.0, 0.0).astype(BF16)
    pre1 = _dot(tri, oh1.astype(BF16))
    pre2 = _dot(tri, oh2.astype(BF16))
    tot1 = jnp.sum(oh1, axis=0, keepdims=True)
    tot2 = jnp.sum(oh2, axis=0, keepdims=True)
    seen = cnt_ref[...]
    r1 = jnp.sum(oh1 * (pre1 + seen), axis=-1, keepdims=True)
    r2 = jnp.sum(oh2 * (pre2 + seen + tot1), axis=-1, keepdims=True)
    cnt_ref[...] = seen + tot1 + tot2
    sel = jnp.zeros(logits.shape, F32)
    for k, val in ((SEL_I1, i1), (SEL_I2, i2), (SEL_G1, g1), (SEL_G2, g2), (SEL_R1, r1), (SEL_R2, r2)):
        sel = jnp.where(lane == k, val, sel)
    sel_ref[...] = sel


def _router(x, g, mod, wr_pad, S):
    T, D = x.shape
    tm = 512
    return pl.pallas_call(
        _router_kernel,
        grid=(T // tm,),
        in_specs=[
            pl.BlockSpec((tm, D), lambda i: (i, 0)),
            pl.BlockSpec((1, D), lambda i: (0, 0)),
            _mod_spec(3, tm, S),
            _mod_spec(4, tm, S),
            pl.BlockSpec((D, LANES), lambda i: (0, 0)),
        ],
        out_specs=[pl.BlockSpec((tm, D), lambda i: (i, 0)),
                   pl.BlockSpec((tm, LANES), lambda i: (i, 0)),
                   pl.BlockSpec((1, LANES), lambda i: (0, 0))],
        out_shape=[jax.ShapeDtypeStruct((T, D), F32),
                   jax.ShapeDtypeStruct((T, LANES), F32),
                   jax.ShapeDtypeStruct((1, LANES), F32)],
        compiler_params=_cparams("arbitrary"),
        name="router",
    )(x, g.reshape(1, D), mod, mod, wr_pad)


def _route_tables(sel, cnt, T, n_tiles):
    i1 = sel[:, SEL_I1].astype(jnp.int32)
    i2 = sel[:, SEL_I2].astype(jnp.int32)
    r1 = sel[:, SEL_R1].astype(jnp.int32)
    r2 = sel[:, SEL_R2].astype(jnp.int32)
    counts = cnt[0, :N_EXPERTS].astype(jnp.int32)
    tiles_e = (counts + MOE_TM - 1) // MOE_TM
    ends = jnp.cumsum(tiles_e)
    offs = (ends - tiles_e) * MOE_TM
    pos1 = offs[i1] + r1
    pos2 = offs[i2] + r2
    tile = jnp.arange(n_tiles, dtype=jnp.int32)
    n_used = ends[-1]
    valid = (tile < n_used).astype(jnp.int32)
    expert_of = jnp.sum((tile[:, None] >= ends[None, :]).astype(jnp.int32), axis=1)
    expert_of = expert_of[jnp.minimum(tile, n_used - 1)]
    tok = jnp.arange(T, dtype=jnp.int32)
    src = jnp.zeros((n_tiles * MOE_TM,), jnp.int32).at[pos1].set(tok).at[pos2].set(tok)
    return pos1, pos2, src, expert_of, valid


def _row_copy(src_hbm, row, dst_ref, r, sem):
    return pltpu.make_async_copy(src_hbm.at[pl.ds(row, 1)], dst_ref.at[pl.ds(r, 1)], sem)


ROW_DMA_UNROLL = 8


def _start_rows(idx_ref, src_hbm, dst_ref, sem, first, count, alternate=True):
    def body(g, c):
        for u in range(ROW_DMA_UNROLL):
            r = first + g * ROW_DMA_UNROLL + u
            _row_copy(src_hbm, idx_ref[r], dst_ref, r, sem).start(priority=u % 2 if alternate else 0)
        return c

    lax.fori_loop(0, count // ROW_DMA_UNROLL, body, 0)


def _wait_rows(src_hbm, dst_ref, sem):
    def body(g, c):
        for u in range(ROW_DMA_UNROLL):
            _row_copy(src_hbm, 0, dst_ref, g * ROW_DMA_UNROLL + u, sem).wait()
        return c

    lax.fori_loop(0, dst_ref.shape[0] // ROW_DMA_UNROLL, body, 0)


def _moe_ffn_kernel(te_ref, va_ref, src0_ref, srcn_ref, h_hbm, wg_ref, wu_ref, wd_ref, o_ref,
                    xbuf_ref, hb_ref, sems, *, n_tiles, issue_steps):
    i = pl.program_id(0)
    j = pl.program_id(1)
    tm = hb_ref.shape[0]
    per_step = tm // issue_steps

    @pl.when((i == 0) & (j == 0))
    def _():
        _start_rows(src0_ref, h_hbm, xbuf_ref.at[0], sems.at[0], 0, tm, alternate=False)

    @pl.when((i + 1 < n_tiles) & (j < issue_steps))
    def _():
        nxt = (i + 1) % 2
        _start_rows(srcn_ref, h_hbm, xbuf_ref.at[nxt], sems.at[nxt], j * per_step, per_step, alternate=False)

    @pl.when(j == 0)
    def _():
        cur = i % 2
        _wait_rows(h_hbm, xbuf_ref.at[cur], sems.at[cur])
        hb_ref[...] = xbuf_ref[cur].astype(BF16)
        o_ref[...] = jnp.zeros_like(o_ref)

    @pl.when(va_ref[i] == 1)
    def _():
        o_ref[...] += _swiglu_tile(hb_ref[...], wg_ref[...], wu_ref[...], wd_ref[...])


def _moe_ffn(hn, src, expert_of, valid, wg, wu, wd):
    T, D = hn.shape
    P = src.shape[0]
    E, _, F = wg.shape
    tm, tf = MOE_TM, 512
    nj = F // tf
    n_tiles = P // tm
    issue_steps = 8
    assert nj >= issue_steps and tm % (issue_steps * ROW_DMA_UNROLL) == 0

    def wcol(i, j, te, va):
        return jnp.where(va[i] == 1, j, nj - 1)

    grid_spec = pltpu.PrefetchScalarGridSpec(
        num_scalar_prefetch=2,
        grid=(n_tiles, nj),
        in_specs=[
            pl.BlockSpec((tm,), lambda i, j, te, va: (0,), memory_space=pltpu.SMEM),
            pl.BlockSpec((tm,), lambda i, j, te, va: (jnp.minimum(i + 1, n_tiles - 1),), memory_space=pltpu.SMEM),
            pl.BlockSpec(memory_space=pl.ANY),
            pl.BlockSpec((None, D, tf), lambda i, j, te, va: (te[i], 0, wcol(i, j, te, va))),
            pl.BlockSpec((None, D, tf), lambda i, j, te, va: (te[i], 0, wcol(i, j, te, va))),
            pl.BlockSpec((None, tf, D), lambda i, j, te, va: (te[i], wcol(i, j, te, va), 0)),
        ],
        out_specs=pl.BlockSpec((tm, D), lambda i, j, te, va: (i, 0)),
        scratch_shapes=[pltpu.VMEM((2, tm, D), F32), pltpu.VMEM((tm, D), BF16), pltpu.SemaphoreType.DMA((2,))],
    )
    return pl.pallas_call(
        functools.partial(_moe_ffn_kernel, n_tiles=n_tiles, issue_steps=issue_steps),
        grid_spec=grid_spec,
        out_shape=jax.ShapeDtypeStruct((P, D), F32),
        compiler_params=_cparams("arbitrary", "arbitrary"),
        name="moe_ffn",
    )(expert_of, valid, src, src, hn, wg, wu, wd)


def _combine_kernel(p1_ref, p2_ref, sel_ref, x_ref, gt_ref, ys_hbm, o_ref, y1_ref, y2_ref, sem1, sem2):
    n = y1_ref.shape[0]
    _start_rows(p1_ref, ys_hbm, y1_ref, sem1, 0, n)
    _start_rows(p2_ref, ys_hbm, y2_ref, sem2, 0, n)
    _wait_rows(ys_hbm, y1_ref, sem1)
    _wait_rows(ys_hbm, y2_ref, sem2)
    sel = sel_ref[...]
    lane = lax.broadcasted_iota(jnp.int32, sel.shape, 1)
    g1 = _lane_pick(sel, lane, SEL_G1)
    g2 = _lane_pick(sel, lane, SEL_G2)
    o_ref[...] = x_ref[...] + gt_ref[...] * (g1 * y1_ref[...] + g2 * y2_ref[...])


def _combine(ys, pos1, pos2, sel, x, mod, S):
    T, D = x.shape
    tm = 256
    idx = pl.BlockSpec((tm,), lambda i: (i,), memory_space=pltpu.SMEM)
    return pl.pallas_call(
        _combine_kernel,
        grid=(T // tm,),
        in_specs=[idx, idx,
                  pl.BlockSpec((tm, LANES), lambda i: (i, 0)),
                  pl.BlockSpec((tm, D), lambda i: (i, 0)),
                  _mod_spec(5, tm, S),
                  pl.BlockSpec(memory_space=pl.ANY)],
        out_specs=pl.BlockSpec((tm, D), lambda i: (i, 0)),
        out_shape=jax.ShapeDtypeStruct((T, D), F32),
        scratch_shapes=[pltpu.VMEM((tm, D), F32), pltpu.VMEM((tm, D), F32),
                        pltpu.SemaphoreType.DMA(()), pltpu.SemaphoreType.DMA(())],
        compiler_params=_cparams("arbitrary"),
        name="moe_combine",
    )(pos1, pos2, sel, x, mod, ys)


def _moe(x, g, mod, wr_pad, wg, wu, wd, S):
    T = x.shape[0]
    n_tiles = (2 * T) // MOE_TM + N_EXPERTS
    hn, sel, cnt = _router(x, g, mod, wr_pad, S)
    pos1, pos2, src, expert_of, valid = _route_tables(sel, cnt, T, n_tiles)
    ys = _moe_ffn(hn, src, expert_of, valid, wg, wu, wd)
    return _combine(ys, pos1, pos2, sel, x, mod, S)


def _cast_kernel(w_ref, o_ref):
    o_ref[...] = w_ref[...].astype(BF16)


def _cast_layer(w, l):
    if w.ndim == 3:
        return _cast_layer(w[:, None], l)[0]
    _, E, R, C = w.shape
    tr = 256
    return pl.pallas_call(
        _cast_kernel,
        grid=(E, R // tr),
        in_specs=[pl.BlockSpec((None, None, tr, C), lambda e, r: (l, e, r, 0))],
        out_specs=pl.BlockSpec((None, tr, C), lambda e, r: (e, r, 0)),
        out_shape=jax.ShapeDtypeStruct((E, R, C), BF16),
        compiler_params=_cparams("parallel", "parallel"),
        name="cast_bf16",
    )(w)


def _final_kernel(x_ref, g_ref, o_ref):
    x = x_ref[...]
    o_ref[...] = x * lax.rsqrt(jnp.mean(x * x, axis=-1, keepdims=True) + NORM_EPS) * g_ref[...]


def _final_norm(x, g):
    T, D = x.shape
    tm = 1024
    return pl.pallas_call(
        _final_kernel,
        grid=(T // tm,),
        in_specs=[pl.BlockSpec((tm, D), lambda i: (i, 0)), pl.BlockSpec((1, D), lambda i: (0, 0))],
        out_specs=pl.BlockSpec((tm, D), lambda i: (i, 0)),
        out_shape=jax.ShapeDtypeStruct((T, D), F32),
        compiler_params=_cparams("parallel"),
        name="final_norm",
    )(x, g.reshape(1, D))


def _prep_layer(l, p):
    t5 = p["t5_table"]
    sw_tab = jnp.stack([t5[:, h] for h in SW_PERM])
    dil_tab = t5[:, SW_HEADS:].reshape(T5_BUCKETS, DIL_GROUPS, DIL_HEADS).transpose(1, 2, 0)
    w_in = p["w_in"][l]
    out = {
        "w_main": _take(w_in, _MAIN_RUNS, 1),
        "w_dil": [_take(w_in, _DIL_RUNS[g], 1) for g in range(DIL_GROUPS)],
        "w_uq": _take(p["mla_w_uq"][l], _UQ_RUNS, 1),
        "w_ukv": _take(p["mla_w_ukv"][l], _UKV_RUNS, 1),
        "w_out": _take(p["w_out"][l], _OUT_RUNS, 0),
        "na_bias": _na_bias(p["na_rpb"][l]),
        "sw_bias": _band_bias(sw_tab, SW_RADIUS, 1, LANES, LANES + 2 * SW_RADIUS),
        "sinks": jnp.stack([p["sw_sinks"][l][h] for h in SW_PERM]) * LOG2E,
        "dil_tab": dil_tab,
    }
    if l % 2 == 0:
        out["wg"] = _cast_layer(p["ffn_w_gate"], l // 2)
        out["wu"] = _cast_layer(p["ffn_w_up"], l // 2)
        out["wd"] = _cast_layer(p["ffn_w_down"], l // 2)
    else:
        out["wr"] = jnp.pad(p["moe_w_router"][l // 2], ((0, 0), (0, LANES - N_EXPERTS)))
        out["wg"] = _cast_layer(p["moe_w_gate"], l // 2)
        out["wu"] = _cast_layer(p["moe_w_up"], l // 2)
        out["wd"] = _cast_layer(p["moe_w_down"], l // 2)
    return out


def _token_mix(proj, hn, lw, p, l, B, S, rope):
    o_a = _na_attn(proj, lw["na_bias"], B, S)
    qm, km, vm = _mla_up(proj, rope[0], rope[1], p["mla_g_q"][l], p["mla_g_kv"][l], lw["w_uq"], lw["w_ukv"], S)
    o_b = _mla_attn(qm, km, vm, B, S)
    o_c = _sw_attn(proj, lw["sw_bias"], lw["sinks"], B, S)
    outs = []
    for g, dil in enumerate(DIL_DILATIONS):
        L = S // dil
        bias = _band_bias(lw["dil_tab"][g], DIL_SIDE, dil, LANES, min(L, LANES + 2 * DIL_SIDE))
        outs.append(_dil_attn(_dil_proj(hn, lw["w_dil"][g], B, S, dil), bias, dil))
    o_d = _merge(outs, B, S)
    return o_a, o_b, o_c, o_d


def _run_trunk(x, mods, layer_w, p, B, S):
    T = B * S
    x = x.reshape(T, D_MODEL)
    rope = _rope_tables(S)
    for l in range(DEPTH):
        lw = layer_w[l]
        mod = mods[l]
        proj, hn = _in_proj(x, p["g_mix"][l], mod, lw["w_main"], S)
        o_a, o_b, o_c, o_d = _token_mix(proj, hn, lw, p, l, B, S, rope)
        x = _out_proj(o_a, o_b, o_c, o_d, lw["w_out"], x, mod, S)
        if l % 2 == 0:
            x = _ffn(x, p["g_ffn"][l], mod, lw["wg"], lw["wu"], lw["wd"], S)
        else:
            x = _moe(x, p["g_ffn"][l], mod, lw["wr"], lw["wg"], lw["wu"], lw["wd"], S)
    return _final_norm(x, p["g_final"]).reshape(B, S, D_MODEL)


def kernel(x_prompt, x_sample, c_prompt, c_sample, w_ada, b_ada, g_mix, g_ffn, w_in, mla_g_q, mla_g_kv, mla_w_uq, mla_w_ukv, na_rpb, sw_sinks, t5_table, w_out, ffn_w_gate, ffn_w_up, ffn_w_down, moe_w_router, moe_w_gate, moe_w_up, moe_w_down, g_final):
    p = dict(g_mix=g_mix, g_ffn=g_ffn, w_in=w_in, mla_g_q=mla_g_q, mla_g_kv=mla_g_kv, mla_w_uq=mla_w_uq,
             mla_w_ukv=mla_w_ukv, na_rpb=na_rpb, sw_sinks=sw_sinks, t5_table=t5_table, w_out=w_out,
             ffn_w_gate=ffn_w_gate, ffn_w_up=ffn_w_up, ffn_w_down=ffn_w_down, moe_w_router=moe_w_router,
             moe_w_gate=moe_w_gate, moe_w_up=moe_w_up, moe_w_down=moe_w_down, g_final=g_final)
    Bp, Sp, _ = x_prompt.shape
    Bs, Ss, _ = x_sample.shape
    rows = 16
    c_pad = jnp.concatenate([c_prompt, c_sample, jnp.zeros((rows - Bp - Bs, D_MODEL), F32)], axis=0)
    mod_all = _ada(c_pad, w_ada, b_ada)
    mods_p = [mod_all[l, :Bp].reshape(Bp, 6, 1, D_MODEL) for l in range(DEPTH)]
    mods_s = [mod_all[l, Bp:Bp + Bs].reshape(Bs, 6, 1, D_MODEL) for l in range(DEPTH)]
    layer_w = [_prep_layer(l, p) for l in range(DEPTH)]
    y_prompt = _run_trunk(x_prompt, mods_p, layer_w, p, Bp, Sp)
    y_sample = _run_trunk(x_sample, mods_s, layer_w, p, Bs, Ss)
    return (y_prompt, y_sample)
```

```python
import functools
import math

import numpy as np
import jax
import jax.numpy as jnp
from jax import lax
from jax.experimental import pallas as pl
from jax.experimental.pallas import tpu as pltpu

F32 = jnp.float32
BF16 = jnp.bfloat16

D_MODEL = 2048
DEPTH = 4
HEAD_DIM = 64
GRID_W = 64
NA_HEADS = 8
NA_ROWS = 8
NA_COLS = 16
MLA_HEADS = 8
MLA_Q_RANK = 512
MLA_KV_RANK = 256
MLA_NOPE = 64
MLA_ROPE = 32
MLA_V = 64
ROPE_THETA = 10000.0
SW_HEADS = 8
SW_KV_HEADS = 2
SW_RADIUS = 128
DIL_DILATIONS = (1, 4, 16)
DIL_GROUPS = 3
DIL_HEADS = 8
DIL_SIDE = 64
T5_BUCKETS = 32
T5_MAX_DIST = 1024
N_EXPERTS = 8
D_FF = 5632
NORM_EPS = 1e-6

LANES = 128
NEG = -1e30
LOG2E = math.log2(math.e)
VMEM_LIMIT = 56 * 1024 * 1024
MOE_TM = 512
MLA_SUM_ROWS = 16

NA_IN = 3 * NA_HEADS * HEAD_DIM
MLA_IN = MLA_Q_RANK + MLA_KV_RANK + MLA_ROPE
SW_IN = (SW_HEADS + 2 * SW_KV_HEADS) * HEAD_DIM
DIL_IN = 3 * DIL_GROUPS * DIL_HEADS * HEAD_DIM
P_NA = 0
P_CQ = 1536
P_CKV = 2048
P_KR = 2304
P_KRR = 2432
P_SWQ = 2560
P_SWK = 3072
P_SWV = 3200
P_MAIN = 3584
GW = DIL_HEADS * HEAD_DIM
SW_PERM = (0, 4, 1, 5, 2, 6, 3, 7)


def _cparams(*sem):
    return pltpu.CompilerParams(dimension_semantics=sem, vmem_limit_bytes=VMEM_LIMIT)


def _nt_dot(a, b):
    return lax.dot_general(a, b, (((1,), (1,)), ((), ())), preferred_element_type=F32)


def _dot(a, b):
    return jnp.dot(a, b, preferred_element_type=F32)


def _rot_half_cols(base):
    half = MLA_ROPE // 2
    src = [base + half + j for j in range(half)] + [base + j for j in range(half)]
    sgn = [-1.0] * half + [1.0] * half
    return src, sgn


def _main_colmap():
    src = np.zeros((P_MAIN,), np.int32)
    mul = np.zeros((P_MAIN,), np.float32)
    qs = HEAD_DIM ** -0.5 * LOG2E
    for j in range(NA_IN):
        src[P_NA + j] = j
        mul[P_NA + j] = qs if j < NA_HEADS * HEAD_DIM else 1.0
    b0 = NA_IN
    for j in range(MLA_Q_RANK + MLA_KV_RANK):
        src[P_CQ + j] = b0 + j
        mul[P_CQ + j] = 1.0
    kr0 = b0 + MLA_Q_RANK + MLA_KV_RANK
    rsrc, rsgn = _rot_half_cols(kr0)
    for rep in range(2):
        for j in range(MLA_ROPE):
            src[P_KR + rep * MLA_ROPE + j] = kr0 + j
            mul[P_KR + rep * MLA_ROPE + j] = 1.0
            src[P_KRR + rep * MLA_ROPE + j] = rsrc[j]
            mul[P_KRR + rep * MLA_ROPE + j] = rsgn[j]
    c0 = NA_IN + MLA_IN
    for hh, h in enumerate(SW_PERM):
        for d in range(HEAD_DIM):
            src[P_SWQ + hh * HEAD_DIM + d] = c0 + h * HEAD_DIM + d
            mul[P_SWQ + hh * HEAD_DIM + d] = qs
    for j in range(2 * SW_KV_HEADS * HEAD_DIM):
        src[P_SWK + j] = c0 + SW_HEADS * HEAD_DIM + j
        mul[P_SWK + j] = 1.0
    return src, mul


def _dil_colmap(g):
    d0 = NA_IN + MLA_IN + SW_IN
    src = np.zeros((3 * GW,), np.int32)
    mul = np.ones((3 * GW,), np.float32)
    for t in range(3):
        for j in range(GW):
            src[t * GW + j] = d0 + (t * DIL_GROUPS + g) * GW + j
    mul[:GW] = HEAD_DIM ** -0.5 * LOG2E
    return src, mul


def _mla_q_colmap():
    n = 3 * 4 * LANES
    src = np.zeros((n,), np.int32)
    mul = np.zeros((n,), np.float32)
    hw = MLA_NOPE + MLA_ROPE
    for h in range(MLA_HEADS):
        p, a = divmod(h, 2)
        for d in range(MLA_NOPE):
            src[p * LANES + a * MLA_NOPE + d] = h * hw + d
            mul[p * LANES + a * MLA_NOPE + d] = 1.0
        rsrc, rsgn = _rot_half_cols(h * hw + MLA_NOPE)
        for j in range(MLA_ROPE):
            ca = 4 * LANES + p * LANES + a * MLA_ROPE + j
            cb = 8 * LANES + p * LANES + a * MLA_ROPE + j
            src[ca] = h * hw + MLA_NOPE + j
            mul[ca] = 1.0
            src[cb] = rsrc[j]
            mul[cb] = rsgn[j]
    return src, mul


def _mla_kv_colmap():
    n = 2 * 4 * LANES
    src = np.zeros((n,), np.int32)
    hw = MLA_NOPE + MLA_V
    for h in range(MLA_HEADS):
        for d in range(MLA_NOPE):
            src[h * MLA_NOPE + d] = h * hw + d
            src[4 * LANES + h * MLA_V + d] = h * hw + MLA_NOPE + d
    return src, np.ones((n,), np.float32)


def _out_proj_rowmap():
    src = np.arange(4 * 512, dtype=np.int32)
    for hh, h in enumerate(SW_PERM):
        for d in range(HEAD_DIM):
            src[1024 + hh * HEAD_DIM + d] = 1024 + h * HEAD_DIM + d
    return src, np.ones((4 * 512,), np.float32)


def _runs(colmap):
    src, mul = colmap
    runs, i, n = [], 0, len(src)
    while i < n:
        j = i + 1
        if mul[i] == 0.0:
            while j < n and mul[j] == 0.0:
                j += 1
            runs.append((None, j - i, 0.0))
        else:
            while j < n and mul[j] == mul[i] and src[j] == src[j - 1] + 1:
                j += 1
            runs.append((int(src[i]), j - i, float(mul[i])))
        i = j
    return runs


_MAIN_RUNS = _runs(_main_colmap())
_DIL_RUNS = [_runs(_dil_colmap(g)) for g in range(DIL_GROUPS)]
_UQ_RUNS = _runs(_mla_q_colmap())
_UKV_RUNS = _runs(_mla_kv_colmap())
_OUT_RUNS = _runs(_out_proj_rowmap())


def _take(w, runs, axis):
    parts = []
    for start, n, m in runs:
        if start is None:
            shape = list(w.shape)
            shape[axis] = n
            parts.append(jnp.zeros(shape, w.dtype))
        else:
            piece = lax.slice_in_dim(w, start, start + n, axis=axis)
            parts.append(piece if m == 1.0 else piece * m)
    return jnp.concatenate(parts, axis=axis).astype(BF16)


def _one_hot(idx, n):
    return (idx[..., None] == jnp.arange(n, dtype=idx.dtype)).astype(F32)


def _select(one_hot, table, spec):
    return jnp.einsum(spec, one_hot, table, precision=lax.Precision.HIGHEST, preferred_element_type=F32)


def _t5_bucket(rel):
    half = T5_BUCKETS // 2
    max_exact = half // 2
    side = jnp.where(rel > 0, half, 0)
    n = jnp.abs(rel)
    nf = jnp.maximum(n, 1).astype(F32)
    large = max_exact + (jnp.log(nf / max_exact) / math.log(T5_MAX_DIST / max_exact) * (half - max_exact)).astype(jnp.int32)
    large = jnp.minimum(large, half - 1)
    return side + jnp.where(n < max_exact, n, large)


def _pair_rows(b):
    H, V, q, k = b.shape
    return b.reshape(H // 2, 2, V, q, k).transpose(0, 2, 1, 3, 4).reshape(H // 2, V, 2 * q, k)


def _band_bias(table_hb, radius, dilation, tq, kw):
    qa = np.arange(tq)[:, None]
    kc = np.arange(kw)[None, :]
    outs = []
    for v in range(3):
        rel = kc - qa - v * radius
        valid = jnp.asarray(np.abs(rel) <= radius)
        bucket = _t5_bucket(jnp.asarray(rel * dilation, jnp.int32))
        b = _select(_one_hot(bucket, T5_BUCKETS), table_hb, "qkb,hb->hqk")
        outs.append(jnp.where(valid[None], b * LOG2E, NEG))
    return _pair_rows(jnp.stack(outs, axis=1).astype(F32))


def _na_bias(rpb):
    c = np.arange(GRID_W)[:, None]
    kc = np.arange(GRID_W)[None, :]
    col_start = np.clip(c - NA_COLS // 2, 0, GRID_W - NA_COLS)
    valid = (kc >= col_start) & (kc < col_start + NA_COLS)
    cidx = np.clip(kc - c + NA_COLS - 1, 0, 2 * NA_COLS - 2)
    d = np.arange(NA_ROWS)[:, None]
    kr = np.arange(NA_ROWS)[None, :]
    ridx = kr - d + NA_ROWS - 1
    rsel = _one_hot(jnp.asarray(ridx.reshape(-1)), 2 * NA_ROWS - 1)
    csel = _one_hot(jnp.asarray(cidx.reshape(-1)), 2 * NA_COLS - 1)
    b = _select(rsel, _select(csel, rpb, "cj,hij->hic"), "ri,hic->hrc")
    b = b.reshape(NA_HEADS, NA_ROWS, NA_ROWS, GRID_W, GRID_W)
    b = jnp.where(jnp.asarray(valid)[None, None, None], b * LOG2E, NEG)
    b = b.transpose(0, 1, 3, 2, 4)
    return _pair_rows(b.reshape(NA_HEADS, NA_ROWS, GRID_W, NA_ROWS * GRID_W).astype(F32))


def _rope_tables(S):
    inv = jnp.power(jnp.float32(ROPE_THETA), -jnp.arange(0, MLA_ROPE, 2, dtype=F32) / MLA_ROPE)
    ang = jnp.arange(S, dtype=F32)[:, None] * inv[None, :]
    cos, sin = jnp.cos(ang), jnp.sin(ang)
    z = jnp.zeros((S, LANES - 2 * MLA_ROPE), F32)
    cos_l = jnp.concatenate([cos, cos, cos, cos, z], axis=1)
    sin_l = jnp.concatenate([sin, sin, sin, sin, z], axis=1)
    return cos_l, sin_l


def _ada_kernel(c_ref, w_ref, b_ref, o_ref):
    c = c_ref[...]
    cs = (c * jax.nn.sigmoid(c)).astype(BF16)
    o_ref[...] = _dot(cs, w_ref[...].astype(BF16)) + b_ref[...]


def _ada(c_pad, w_ada, b_ada):
    L, D, N = w_ada.shape
    R = c_pad.shape[0]
    tn = 1024
    return pl.pallas_call(
        _ada_kernel,
        grid=(L, N // tn),
        in_specs=[
            pl.BlockSpec((R, D), lambda l, j: (0, 0)),
            pl.BlockSpec((None, D, tn), lambda l, j: (l, 0, j)),
            pl.BlockSpec((None, 1, tn), lambda l, j: (l, 0, j)),
        ],
        out_specs=pl.BlockSpec((None, R, tn), lambda l, j: (l, 0, j)),
        out_shape=jax.ShapeDtypeStruct((L, R, N), F32),
        compiler_params=_cparams("parallel", "parallel"),
        name="ada",
    )(c_pad, w_ada, b_ada.reshape(L, 1, N))


def _norm_mod(x, g, sh, sc):
    y = x * lax.rsqrt(jnp.mean(x * x, axis=-1, keepdims=True) + NORM_EPS) * g
    return y * (1.0 + sc) + sh


def _mod_spec(k, tm, S):
    return pl.BlockSpec((None, None, 1, D_MODEL), lambda i, *_: ((i * tm) // S, k, 0, 0))


def _lane_half_mask(shape):
    return lax.broadcasted_iota(jnp.int32, shape, len(shape) - 1) < HEAD_DIM


def _stack_heads(q, first):
    zero = jnp.zeros_like(q)
    return jnp.concatenate([jnp.where(first, q, zero), jnp.where(first, zero, q)], axis=0)


def _in_kernel(x_ref, g_ref, sh_ref, sc_ref, w_ref, o_ref, hn_ref):
    @pl.when(pl.program_id(1) == 0)
    def _():
        hn_ref[...] = _norm_mod(x_ref[...], g_ref[...], sh_ref[...], sc_ref[...]).astype(BF16)

    o_ref[...] = _dot(hn_ref[...], w_ref[...]).astype(BF16)


def _in_proj(x, g, mod, w, S):
    T, D = x.shape
    N = w.shape[1]
    tm, tn = 512, N // 2
    return pl.pallas_call(
        _in_kernel,
        grid=(T // tm, N // tn),
        in_specs=[
            pl.BlockSpec((tm, D), lambda i, j: (i, 0)),
            pl.BlockSpec((1, D), lambda i, j: (0, 0)),
            _mod_spec(0, tm, S),
            _mod_spec(1, tm, S),
            pl.BlockSpec((D, tn), lambda i, j: (0, j)),
        ],
        out_specs=[pl.BlockSpec((tm, tn), lambda i, j: (i, j)),
                   pl.BlockSpec((tm, D), lambda i, j: (i, 0))],
        out_shape=[jax.ShapeDtypeStruct((T, N), BF16), jax.ShapeDtypeStruct((T, D), BF16)],
        compiler_params=_cparams("parallel", "arbitrary"),
        name="in_proj",
    )(x, g.reshape(1, D), mod, mod, w)


def _dil_proj_kernel(h_ref, w_ref, o_ref, *scratch, dil):
    r = _dot(h_ref[...], w_ref[...])
    if dil == 1:
        o_ref[0] = r.astype(BF16)
    else:
        r_ref, = scratch
        n = r_ref.shape[1] // dil
        for jb in range(r_ref.shape[0]):
            cs = slice(jb * LANES, (jb + 1) * LANES)
            r_ref[jb] = r[:, cs]
            for c in range(dil):
                o_ref[c, :, cs] = r_ref[jb, pl.ds(c, n, stride=dil), :].astype(BF16)


def _dil_proj(hn, w, B, S, dil):
    T, D = hn.shape
    N = w.shape[1]
    tm = 1024
    nsb = S // tm
    scratch = [] if dil == 1 else [pltpu.VMEM((N // LANES, tm, LANES), F32)]
    return pl.pallas_call(
        functools.partial(_dil_proj_kernel, dil=dil),
        grid=(T // tm,),
        in_specs=[pl.BlockSpec((tm, D), lambda i: (i, 0)), pl.BlockSpec((D, N), lambda i: (0, 0))],
        out_specs=pl.BlockSpec((None, dil, tm // dil, N), lambda i: (i // nsb, 0, i % nsb, 0)),
        out_shape=jax.ShapeDtypeStruct((B, dil, S // dil, N), BF16),
        scratch_shapes=scratch,
        compiler_params=_cparams("parallel"),
        name="dil_proj_d%d" % dil,
    )(hn, w)


def _softmax_pv(s, v):
    m = jnp.max(s, axis=-1, keepdims=True)
    p = jnp.exp2(s - m)
    l = jnp.sum(p, axis=-1, keepdims=True)
    return _dot(p.astype(BF16), v) / l


def _issue_ahead(n_items, scores, finish):
    cur = scores(0)
    for i in range(n_items):
        nxt = scores(i + 1) if i + 1 < n_items else None
        finish(i, cur)
        cur = nxt


def _na_kernel(q_ref, k_ref, v_ref, bias_ref, o_ref, *, rows, qrows):
    rb = pl.program_id(1)
    kwin = NA_ROWS * GRID_W
    n_pairs = NA_HEADS // 2
    lo = _lane_half_mask((GRID_W, LANES))

    def window(qr):
        r = rb * qrows + qr
        r_start = jnp.clip(r - NA_ROWS // 2, 0, rows - NA_ROWS)
        return r - r_start, pl.multiple_of(r_start * GRID_W, GRID_W)

    def scores(qr):
        d, k0 = window(qr)
        out = []
        for p in range(n_pairs):
            cs = slice(p * LANES, (p + 1) * LANES)
            q = q_ref[qr * GRID_W:(qr + 1) * GRID_W, cs]
            out.append(_nt_dot(_stack_heads(q, lo), k_ref[pl.ds(k0, kwin), cs]) + bias_ref[p, d])
        return out

    def finish(qr, ss):
        _, k0 = window(qr)
        for p in range(n_pairs):
            cs = slice(p * LANES, (p + 1) * LANES)
            o = _softmax_pv(ss[p], v_ref[pl.ds(k0, kwin), cs])
            o_ref[qr * GRID_W:(qr + 1) * GRID_W, cs] = jnp.where(lo, o[:GRID_W], o[GRID_W:]).astype(BF16)

    _issue_ahead(qrows, scores, finish)


def _na_attn(proj, bias, B, S):
    T = B * S
    rows = S // GRID_W
    qrows = 8
    tq = qrows * GRID_W
    nqb = S // tq
    w = NA_HEADS * HEAD_DIM
    kern = functools.partial(_na_kernel, rows=rows, qrows=qrows)
    return pl.pallas_call(
        kern,
        grid=(B, nqb),
        in_specs=[
            pl.BlockSpec((tq, w), lambda b, r: (b * nqb + r, P_NA // w)),
            pl.BlockSpec((S, w), lambda b, r: (b, P_NA // w + 1)),
            pl.BlockSpec((S, w), lambda b, r: (b, P_NA // w + 2)),
            pl.BlockSpec(bias.shape, lambda b, r: (0, 0, 0, 0)),
        ],
        out_specs=pl.BlockSpec((tq, w), lambda b, r: (b * nqb + r, 0)),
        out_shape=jax.ShapeDtypeStruct((T, w), BF16),
        compiler_params=_cparams("parallel", "arbitrary"),
        name="na_attn",
    )(proj, proj, proj, bias)


def _mla_up_kernel(cq_ref, ckv_ref, kr_ref, krr_ref, cos_ref, sin_ref, gq_ref, gkv_ref, wq_ref, wkv_ref,
                   q_ref, k_ref, v_ref):
    scale = (MLA_NOPE + MLA_ROPE) ** -0.5 * LOG2E
    cos = cos_ref[...]
    sin = sin_ref[...]

    def rms(x_ref, g_ref):
        x = x_ref[...].astype(F32)
        return (x * lax.rsqrt(jnp.mean(x * x, axis=-1, keepdims=True) + NORM_EPS) * g_ref[...]).astype(BF16)

    qf = _dot(rms(cq_ref, gq_ref), wq_ref[...]) * scale
    kvf = _dot(rms(ckv_ref, gkv_ref), wkv_ref[...])
    kpe = (kr_ref[...].astype(F32) * cos + krr_ref[...].astype(F32) * sin).astype(BF16)
    for p in range(MLA_HEADS // 2):
        nope = qf[:, p * LANES:(p + 1) * LANES]
        pe = qf[:, (4 + p) * LANES:(5 + p) * LANES] * cos + qf[:, (8 + p) * LANES:(9 + p) * LANES] * sin
        q_ref[:, 2 * p * LANES:(2 * p + 1) * LANES] = nope.astype(BF16)
        q_ref[:, (2 * p + 1) * LANES:(2 * p + 2) * LANES] = pe.astype(BF16)
        k_ref[:, 2 * p * LANES:(2 * p + 1) * LANES] = kvf[:, p * LANES:(p + 1) * LANES].astype(BF16)
        k_ref[:, (2 * p + 1) * LANES:(2 * p + 2) * LANES] = kpe
    v_ref[...] = kvf[:, 4 * LANES:].astype(BF16)


def _mla_up(proj, cos_l, sin_l, g_q, g_kv, wq, wkv, S):
    T = proj.shape[0]
    tm = 512
    nsb = S // tm
    row = lambda i: (i, 0)
    return pl.pallas_call(
        _mla_up_kernel,
        grid=(T // tm,),
        in_specs=[
            pl.BlockSpec((tm, MLA_Q_RANK), lambda i: (i, P_CQ // MLA_Q_RANK)),
            pl.BlockSpec((tm, MLA_KV_RANK), lambda i: (i, P_CKV // MLA_KV_RANK)),
            pl.BlockSpec((tm, LANES), lambda i: (i, P_KR // LANES)),
            pl.BlockSpec((tm, LANES), lambda i: (i, P_KRR // LANES)),
            pl.BlockSpec((tm, LANES), lambda i: (i % nsb, 0)),
            pl.BlockSpec((tm, LANES), lambda i: (i % nsb, 0)),
            pl.BlockSpec((1, MLA_Q_RANK), lambda i: (0, 0)),
            pl.BlockSpec((1, MLA_KV_RANK), lambda i: (0, 0)),
            pl.BlockSpec(wq.shape, lambda i: (0, 0)),
            pl.BlockSpec(wkv.shape, lambda i: (0, 0)),
        ],
        out_specs=[
            pl.BlockSpec((tm, 8 * LANES), row),
            pl.BlockSpec((tm, 8 * LANES), row),
            pl.BlockSpec((tm, 4 * LANES), row),
        ],
        out_shape=[
            jax.ShapeDtypeStruct((T, 8 * LANES), BF16),
            jax.ShapeDtypeStruct((T, 8 * LANES), BF16),
            jax.ShapeDtypeStruct((T, 4 * LANES), BF16),
        ],
        compiler_params=_cparams("parallel"),
        name="mla_up",
    )(proj, proj, proj, proj, cos_l, sin_l, g_q.reshape(1, -1), g_kv.reshape(1, -1), wq, wkv)


def _mla_attn_kernel(q_ref, k_ref, v_ref, o_ref, vt_ref, *, tk):
    tq = q_ref.shape[0]
    nk = vt_ref.shape[0]

    @pl.when(pl.program_id(2) == 0)
    def _():
        for c in range(nk):
            vt_ref[c, :LANES, :] = v_ref[c * tk:(c + 1) * tk, :].astype(F32).T.astype(BF16)
            vt_ref[c, LANES:, :] = jnp.ones((MLA_SUM_ROWS, tk), BF16)

    q = q_ref[...]
    lane = lax.broadcasted_iota(jnp.int32, q.shape, 1)
    first = (lane < MLA_NOPE) | ((lane >= LANES) & (lane < LANES + MLA_ROPE))
    qcat = _stack_heads(q, first)

    def scores(c):
        return _nt_dot(k_ref[c * tk:(c + 1) * tk, :], qcat)

    m = jnp.full((1, 2 * tq), NEG, F32)
    acc = jnp.zeros((LANES + MLA_SUM_ROWS, 2 * tq), F32)
    st = scores(0)
    for c in range(nk):
        st_next = scores(c + 1) if c + 1 < nk else None
        mn = jnp.maximum(m, jnp.max(st, axis=0, keepdims=True))
        p = jnp.exp2(st - mn)
        acc = jnp.exp2(m - mn) * acc + _dot(vt_ref[c], p.astype(BF16))
        m, st = mn, st_next
    o = (acc[:LANES] / acc[LANES:LANES + 1]).T
    lo = _lane_half_mask((tq, LANES))
    o_ref[...] = jnp.where(lo, o[:tq], o[tq:]).astype(BF16)


def _mla_attn(qm, km, vm, B, S):
    T = B * S
    tq, tk = 512, 512
    nqb = S // tq
    kern = functools.partial(_mla_attn_kernel, tk=tk)
    return pl.pallas_call(
        kern,
        grid=(B, MLA_HEADS // 2, nqb),
        in_specs=[
            pl.BlockSpec((tq, 2 * LANES), lambda b, p, i: (b * nqb + i, p)),
            pl.BlockSpec((S, 2 * LANES), lambda b, p, i: (b, p)),
            pl.BlockSpec((S, LANES), lambda b, p, i: (b, p)),
        ],
        out_specs=pl.BlockSpec((tq, LANES), lambda b, p, i: (b * nqb + i, p)),
        out_shape=jax.ShapeDtypeStruct((T, MLA_HEADS * MLA_V), BF16),
        scratch_shapes=[pltpu.VMEM((S // tk, LANES + MLA_SUM_ROWS, tk), BF16)],
        compiler_params=_cparams("parallel", "parallel", "arbitrary"),
        name="mla_attn",
    )(qm, km, vm)


def _band_kernel(*refs, L, radius, tq, kw, n_pairs, shared_kv, with_sink, with_lse):
    if with_sink:
        sink_ref, refs = refs[0], refs[1:]
    q_ref, k_ref, v_ref, bias_ref = refs[:4]
    o_ref = refs[4]
    lse_ref = refs[5] if with_lse else None
    qb = pl.program_id(2)
    nq = q_ref.shape[0] // tq
    lo = _lane_half_mask((tq, LANES))
    second = lax.broadcasted_iota(jnp.int32, (2 * tq, 1), 0) >= tq

    def window(t):
        q0 = (qb * nq + t) * tq
        start = jnp.clip(q0 - radius, 0, L - kw)
        return (q0 - start) // radius, pl.multiple_of(start, min(radius, tq))

    def scores(t):
        var, k0 = window(t)
        out = []
        for p in range(n_pairs):
            cs = slice(p * LANES, (p + 1) * LANES)
            kcs = slice(0, LANES) if shared_kv else cs
            q = q_ref[t * tq:(t + 1) * tq, cs]
            out.append(_nt_dot(_stack_heads(q, lo), k_ref[pl.ds(k0, kw), kcs]) + bias_ref[p, var])
        return out

    def finish(t, ss):
        _, k0 = window(t)
        rs = slice(t * tq, (t + 1) * tq)
        for p in range(n_pairs):
            cs = slice(p * LANES, (p + 1) * LANES)
            kcs = slice(0, LANES) if shared_kv else cs
            s = ss[p]
            m = jnp.max(s, axis=-1, keepdims=True)
            if with_sink:
                sk = jnp.where(second, sink_ref[2 * p + 1], sink_ref[2 * p])
                m = jnp.maximum(m, sk)
            e = jnp.exp2(s - m)
            l = jnp.sum(e, axis=-1, keepdims=True)
            if with_sink:
                l = l + jnp.exp2(sk - m)
            o = _dot(e.astype(BF16), v_ref[pl.ds(k0, kw), kcs]) / l
            if with_lse:
                lse = jnp.broadcast_to(m + jnp.log2(l), (2 * tq, LANES))
                o_ref[rs, cs] = jnp.where(lo, o[:tq], o[tq:])
                lse_ref[rs, cs] = jnp.where(lo, lse[:tq], lse[tq:])
            else:
                o_ref[rs, cs] = jnp.where(lo, o[:tq], o[tq:]).astype(BF16)

    _issue_ahead(nq, scores, finish)


def _sw_attn(proj, bias, sinks, B, S):
    T = proj.shape[0]
    tq, radius = LANES, SW_RADIUS
    kw = tq + 2 * radius
    qblk = 512
    nqb = S // qblk
    wq = SW_HEADS * HEAD_DIM
    kern = functools.partial(_band_kernel, L=S, radius=radius, tq=tq, kw=kw, n_pairs=wq // LANES,
                             shared_kv=True, with_sink=True, with_lse=False)
    return pl.pallas_call(
        kern,
        grid=(B, 1, nqb),
        in_specs=[
            pl.BlockSpec(memory_space=pltpu.SMEM),
            pl.BlockSpec((qblk, wq), lambda b, c, i: (b * nqb + i, P_SWQ // wq)),
            pl.BlockSpec((S, LANES), lambda b, c, i: (b, P_SWK // LANES)),
            pl.BlockSpec((S, LANES), lambda b, c, i: (b, P_SWV // LANES)),
            pl.BlockSpec(bias.shape, lambda b, c, i: (0, 0, 0, 0)),
        ],
        out_specs=pl.BlockSpec((qblk, wq), lambda b, c, i: (b * nqb + i, 0)),
        out_shape=jax.ShapeDtypeStruct((T, wq), BF16),
        compiler_params=_cparams("parallel", "parallel", "arbitrary"),
        name="band_attn_sw",
    )(sinks, proj, proj, proj, bias)


def _dil_attn(qkv, bias, dil):
    B, _, L, _ = qkv.shape
    tq, radius = LANES, DIL_SIDE
    kw = min(L, tq + 2 * radius)
    qblk = min(L, 512)
    nqb = L // qblk
    kern = functools.partial(_band_kernel, L=L, radius=radius, tq=tq, kw=kw, n_pairs=GW // LANES,
                             shared_kv=False, with_sink=False, with_lse=True)
    o_spec = pl.BlockSpec((None, None, qblk, GW), lambda b, c, i: (b, c, i, 0))
    o_shape = jax.ShapeDtypeStruct((B, dil, L, GW), F32)
    return pl.pallas_call(
        kern,
        grid=(B, dil, nqb),
        in_specs=[
            pl.BlockSpec((None, None, qblk, GW), lambda b, c, i: (b, c, i, 0)),
            pl.BlockSpec((None, None, L, GW), lambda b, c, i: (b, c, 0, 1)),
            pl.BlockSpec((None, None, L, GW), lambda b, c, i: (b, c, 0, 2)),
            pl.BlockSpec(bias.shape, lambda b, c, i: (0, 0, 0, 0)),
        ],
        out_specs=[o_spec, o_spec],
        out_shape=[o_shape, o_shape],
        compiler_params=_cparams("parallel", "parallel", "arbitrary"),
        name="band_attn_d%d" % dil,
    )(qkv, qkv, qkv, bias)


def _merge_kernel(o0, l0, o1, l1, o2, l2, o_ref, *scratch, dils):
    for jb in range(GW // LANES):
        cs = slice(jb * LANES, (jb + 1) * LANES)
        vals = []
        for g, (o_in, l_in) in enumerate(((o0, l0), (o1, l1), (o2, l2))):
            d = dils[g]
            if d == 1:
                vals.append((o_in[0, :, cs], l_in[0, :, cs]))
            else:
                so, sl = scratch[2 * (g - 1)], scratch[2 * (g - 1) + 1]
                n = so.shape[1] // d
                for c in range(d):
                    so[jb, pl.ds(c, n, stride=d), :] = o_in[c, :, cs]
                    sl[jb, pl.ds(c, n, stride=d), :] = l_in[c, :, cs]
                vals.append((so[jb], sl[jb]))
        (a, la), (b, lb), (c, lc) = vals
        m = jnp.maximum(jnp.maximum(la, lb), lc)
        wa, wb, wc = jnp.exp2(la - m), jnp.exp2(lb - m), jnp.exp2(lc - m)
        o_ref[:, cs] = ((wa * a + wb * b + wc * c) / (wa + wb + wc)).astype(BF16)


def _merge(outs, B, S):
    T = B * S
    tm = 1024
    nsb = S // tm
    in_specs, args = [], []
    for d, (o, l) in zip(DIL_DILATIONS, outs):
        spec = pl.BlockSpec((None, d, tm // d, GW), lambda i: (i // nsb, 0, i % nsb, 0))
        in_specs += [spec, spec]
        args += [o, l]
    scratch = [pltpu.VMEM((GW // LANES, tm, LANES), F32) for d in DIL_DILATIONS[1:] for _ in range(2)]
    return pl.pallas_call(
        functools.partial(_merge_kernel, dils=DIL_DILATIONS),
        grid=(T // tm,),
        in_specs=in_specs,
        out_specs=pl.BlockSpec((tm, GW), lambda i: (i, 0)),
        out_shape=jax.ShapeDtypeStruct((T, GW), BF16),
        scratch_shapes=scratch,
        compiler_params=_cparams("parallel"),
        name="dil_merge",
    )(*args)


def _out_kernel(oa_ref, ob_ref, oc_ref, od_ref, w_ref, x_ref, gt_ref, o_ref):
    acc = _dot(oa_ref[...], w_ref[0:512, :])
    acc += _dot(ob_ref[...], w_ref[512:1024, :])
    acc += _dot(oc_ref[...], w_ref[1024:1536, :])
    acc += _dot(od_ref[...], w_ref[1536:2048, :])
    o_ref[...] = x_ref[...] + gt_ref[...] * acc


def _out_proj(oa, ob, oc, od, w, x, mod, S):
    T, D = x.shape
    tm = 512
    mix = pl.BlockSpec((tm, 512), lambda i: (i, 0))
    return pl.pallas_call(
        _out_kernel,
        grid=(T // tm,),
        in_specs=[mix, mix, mix, mix,
                  pl.BlockSpec(w.shape, lambda i: (0, 0)),
                  pl.BlockSpec((tm, D), lambda i: (i, 0)),
                  _mod_spec(2, tm, S)],
        out_specs=pl.BlockSpec((tm, D), lambda i: (i, 0)),
        out_shape=jax.ShapeDtypeStruct((T, D), F32),
        compiler_params=_cparams("parallel"),
        name="out_proj",
    )(oa, ob, oc, od, w, x, mod)


def _swiglu_tile(h, wg, wu, wd):
    g = _dot(h, wg)
    u = _dot(h, wu)
    return _dot((g * jax.nn.sigmoid(g) * u).astype(BF16), wd)


def _ffn_kernel(x_ref, g_ref, sh_ref, sc_ref, gt_ref, wg_ref, wu_ref, wd_ref, o_ref, hn_ref, acc_ref):
    j = pl.program_id(1)

    @pl.when(j == 0)
    def _():
        hn_ref[...] = _norm_mod(x_ref[...], g_ref[...], sh_ref[...], sc_ref[...]).astype(BF16)
        acc_ref[...] = jnp.zeros_like(acc_ref)

    acc_ref[...] += _swiglu_tile(hn_ref[...], wg_ref[...], wu_ref[...], wd_ref[...])

    @pl.when(j == pl.num_programs(1) - 1)
    def _():
        o_ref[...] = x_ref[...] + gt_ref[...] * acc_ref[...]


def _ffn(x, g, mod, wg, wu, wd, S):
    T, D = x.shape
    F = wg.shape[1]
    tm, tf = 512, 512
    return pl.pallas_call(
        _ffn_kernel,
        grid=(T // tm, F // tf),
        in_specs=[
            pl.BlockSpec((tm, D), lambda i, j: (i, 0)),
            pl.BlockSpec((1, D), lambda i, j: (0, 0)),
            _mod_spec(3, tm, S),
            _mod_spec(4, tm, S),
            _mod_spec(5, tm, S),
            pl.BlockSpec((D, tf), lambda i, j: (0, j)),
            pl.BlockSpec((D, tf), lambda i, j: (0, j)),
            pl.BlockSpec((tf, D), lambda i, j: (j, 0)),
        ],
        out_specs=pl.BlockSpec((tm, D), lambda i, j: (i, 0)),
        out_shape=jax.ShapeDtypeStruct((T, D), F32),
        scratch_shapes=[pltpu.VMEM((tm, D), BF16), pltpu.VMEM((tm, D), F32)],
        compiler_params=_cparams("parallel", "arbitrary"),
        name="ffn",
    )(x, g.reshape(1, D), mod, mod, mod, wg, wu, wd)


SEL_I1, SEL_I2, SEL_G1, SEL_G2, SEL_R1, SEL_R2 = range(6)


def _lane_pick(x, lane, k):
    return jnp.sum(jnp.where(lane == k, x, 0.0), axis=-1, keepdims=True)


def _router_kernel(x_ref, g_ref, sh_ref, sc_ref, wr_ref, hn_ref, sel_ref, cnt_ref):
    @pl.when(pl.program_id(0) == 0)
    def _():
        cnt_ref[...] = jnp.zeros_like(cnt_ref)

    h = _norm_mod(x_ref[...], g_ref[...], sh_ref[...], sc_ref[...])
    hn_ref[...] = h
    tm = h.shape[0]
    logits = jnp.dot(h, wr_ref[...], preferred_element_type=F32, precision=lax.Precision.HIGHEST)
    lane = lax.broadcasted_iota(jnp.int32, logits.shape, 1)
    lanef = lane.astype(F32)
    logits = jnp.where(lane < N_EXPERTS, logits, NEG)
    m1 = jnp.max(logits, axis=-1, keepdims=True)
    i1 = jnp.min(jnp.where(logits == m1, lanef, float(LANES)), axis=-1, keepdims=True)
    rest = jnp.where(lanef == i1, NEG, logits)
    m2 = jnp.max(rest, axis=-1, keepdims=True)
    i2 = jnp.min(jnp.where(rest == m2, lanef, float(LANES)), axis=-1, keepdims=True)
    e2 = jnp.exp(m2 - m1)
    g1 = 1.0 / (1.0 + e2)
    g2 = e2 / (1.0 + e2)
    oh1 = jnp.where(lanef == i1, 1.0, 0.0)
    oh2 = jnp.where(lanef == i2, 1.0, 0.0)
    earlier = (lax.broadcasted_iota(jnp.int32, (tm, tm), 1) < lax.broadcasted_iota(jnp.int32, (tm, tm), 0))
    tri = jnp.where(earlier, 1.0, 0.0).astype(BF16)
    pre1 = _dot(tri, oh1.astype(BF16))
    pre2 = _dot(tri, oh2.astype(BF16))
    tot1 = jnp.sum(oh1, axis=0, keepdims=True)
    tot2 = jnp.sum(oh2, axis=0, keepdims=True)
    seen = cnt_ref[...]
    r1 = jnp.sum(oh1 * (pre1 + seen), axis=-1, keepdims=True)
    r2 = jnp.sum(oh2 * (pre2 + seen + tot1), axis=-1, keepdims=True)
    cnt_ref[...] = seen + tot1 + tot2
    sel = jnp.zeros(logits.shape, F32)
    for k, val in ((SEL_I1, i1), (SEL_I2, i2), (SEL_G1, g1), (SEL_G2, g2), (SEL_R1, r1), (SEL_R2, r2)):
        sel = jnp.where(lane == k, val, sel)
    sel_ref[...] = sel


def _router(x, g, mod, wr_pad, S):
    T, D = x.shape
    tm = 512
    return pl.pallas_call(
        _router_kernel,
        grid=(T // tm,),
        in_specs=[
            pl.BlockSpec((tm, D), lambda i: (i, 0)),
            pl.BlockSpec((1, D), lambda i: (0, 0)),
            _mod_spec(3, tm, S),
            _mod_spec(4, tm, S),
            pl.BlockSpec((D, LANES), lambda i: (0, 0)),
        ],
        out_specs=[pl.BlockSpec((tm, D), lambda i: (i, 0)),
                   pl.BlockSpec((tm, LANES), lambda i: (i, 0)),
                   pl.BlockSpec((1, LANES), lambda i: (0, 0))],
        out_shape=[jax.ShapeDtypeStruct((T, D), F32),
                   jax.ShapeDtypeStruct((T, LANES), F32),
                   jax.ShapeDtypeStruct((1, LANES), F32)],
        compiler_params=_cparams("arbitrary"),
        name="router",
    )(x, g.reshape(1, D), mod, mod, wr_pad)


def _route_tables(sel, cnt, T, n_tiles):
    i1 = sel[:, SEL_I1].astype(jnp.int32)
    i2 = sel[:, SEL_I2].astype(jnp.int32)
    r1 = sel[:, SEL_R1].astype(jnp.int32)
    r2 = sel[:, SEL_R2].astype(jnp.int32)
    counts = cnt[0, :N_EXPERTS].astype(jnp.int32)
    tiles_e = (counts + MOE_TM - 1) // MOE_TM
    ends = jnp.cumsum(tiles_e)
    offs = (ends - tiles_e) * MOE_TM
    pos1 = offs[i1] + r1
    pos2 = offs[i2] + r2
    tile = jnp.arange(n_tiles, dtype=jnp.int32)
    n_used = ends[-1]
    valid = (tile < n_used).astype(jnp.int32)
    expert_of = jnp.sum((tile[:, None] >= ends[None, :]).astype(jnp.int32), axis=1)
    expert_of = expert_of[jnp.minimum(tile, n_used - 1)]
    tok = jnp.arange(T, dtype=jnp.int32)
    src = jnp.zeros((n_tiles * MOE_TM,), jnp.int32).at[pos1].set(tok).at[pos2].set(tok)
    return pos1, pos2, src, expert_of, valid


def _row_copy(src_hbm, row, dst_ref, r, sem):
    return pltpu.make_async_copy(src_hbm.at[pl.ds(row, 1)], dst_ref.at[pl.ds(r, 1)], sem)


ROW_DMA_UNROLL = 8


def _start_rows(idx_ref, src_hbm, dst_ref, sem, first, count, alternate=True):
    def body(g, c):
        for u in range(ROW_DMA_UNROLL):
            r = first + g * ROW_DMA_UNROLL + u
            _row_copy(src_hbm, idx_ref[r], dst_ref, r, sem).start(priority=u % 2 if alternate else 0)
        return c

    lax.fori_loop(0, count // ROW_DMA_UNROLL, body, 0)


def _wait_rows(src_hbm, dst_ref, sem):
    def body(g, c):
        for u in range(ROW_DMA_UNROLL):
            _row_copy(src_hbm, 0, dst_ref, g * ROW_DMA_UNROLL + u, sem).wait()
        return c

    lax.fori_loop(0, dst_ref.shape[0] // ROW_DMA_UNROLL, body, 0)


def _moe_ffn_kernel(te_ref, va_ref, src0_ref, srcn_ref, h_hbm, wg_ref, wu_ref, wd_ref, o_ref,
                    xbuf_ref, hb_ref, sems, *, n_tiles, issue_steps):
    i = pl.program_id(0)
    j = pl.program_id(1)
    tm = hb_ref.shape[0]
    per_step = tm // issue_steps

    @pl.when((i == 0) & (j == 0))
    def _():
        _start_rows(src0_ref, h_hbm, xbuf_ref.at[0], sems.at[0], 0, tm, alternate=False)

    busy = va_ref[i] == 1
    fed = (i == 0) | (va_ref[jnp.maximum(i - 1, 0)] == 1)
    feeds = busy & (i + 1 < n_tiles) & (j < issue_steps)

    @pl.when((j == 0) & fed)
    def _():
        cur = i % 2
        _wait_rows(h_hbm, xbuf_ref.at[cur], sems.at[cur])
        hb_ref[...] = xbuf_ref[cur].astype(BF16)

    @pl.when(j == 0)
    def _():
        o_ref[...] = jnp.zeros_like(o_ref)

    def swiglu():
        o_ref[...] += _swiglu_tile(hb_ref[...], wg_ref[...], wu_ref[...], wd_ref[...])

    @pl.when(feeds)
    def _():
        nxt = (i + 1) % 2
        for u in range(per_step):
            r = j * per_step + u
            _row_copy(h_hbm, srcn_ref[r], xbuf_ref.at[nxt], r, sems.at[nxt]).start()
        swiglu()

    @pl.when(busy & jnp.logical_not(feeds))
    def _():
        swiglu()


def _moe_ffn(hn, src, expert_of, valid, wg, wu, wd):
    T, D = hn.shape
    P = src.shape[0]
    E, _, F = wg.shape
    tm, tf = MOE_TM, 512
    nj = F // tf
    n_tiles = P // tm
    issue_steps = 8
    assert nj >= issue_steps and tm % (issue_steps * ROW_DMA_UNROLL) == 0

    def wcol(i, j, te, va):
        return jnp.where(va[i] == 1, j, nj - 1)

    grid_spec = pltpu.PrefetchScalarGridSpec(
        num_scalar_prefetch=2,
        grid=(n_tiles, nj),
        in_specs=[
            pl.BlockSpec((tm,), lambda i, j, te, va: (0,), memory_space=pltpu.SMEM),
            pl.BlockSpec((tm,), lambda i, j, te, va: (jnp.minimum(i + 1, n_tiles - 1),), memory_space=pltpu.SMEM),
            pl.BlockSpec(memory_space=pl.ANY),
            pl.BlockSpec((None, D, tf), lambda i, j, te, va: (te[i], 0, wcol(i, j, te, va))),
            pl.BlockSpec((None, D, tf), lambda i, j, te, va: (te[i], 0, wcol(i, j, te, va))),
            pl.BlockSpec((None, tf, D), lambda i, j, te, va: (te[i], wcol(i, j, te, va), 0)),
        ],
        out_specs=pl.BlockSpec((tm, D), lambda i, j, te, va: (i, 0)),
        scratch_shapes=[pltpu.VMEM((2, tm, D), F32), pltpu.VMEM((tm, D), BF16), pltpu.SemaphoreType.DMA((2,))],
    )
    return pl.pallas_call(
        functools.partial(_moe_ffn_kernel, n_tiles=n_tiles, issue_steps=issue_steps),
        grid_spec=grid_spec,
        out_shape=jax.ShapeDtypeStruct((P, D), F32),
        compiler_params=_cparams("arbitrary", "arbitrary"),
        name="moe_ffn",
    )(expert_of, valid, src, src, hn, wg, wu, wd)


def _combine_kernel(p1_ref, p2_ref, sel_ref, x_ref, gt_ref, ys_hbm, o_ref, y1_ref, y2_ref, sem1, sem2):
    n = y1_ref.shape[0]
    _start_rows(p1_ref, ys_hbm, y1_ref, sem1, 0, n)
    _start_rows(p2_ref, ys_hbm, y2_ref, sem2, 0, n)
    _wait_rows(ys_hbm, y1_ref, sem1)
    _wait_rows(ys_hbm, y2_ref, sem2)
    sel = sel_ref[...]
    lane = lax.broadcasted_iota(jnp.int32, sel.shape, 1)
    g1 = _lane_pick(sel, lane, SEL_G1)
    g2 = _lane_pick(sel, lane, SEL_G2)
    o_ref[...] = x_ref[...] + gt_ref[...] * (g1 * y1_ref[...] + g2 * y2_ref[...])


def _combine(ys, pos1, pos2, sel, x, mod, S):
    T, D = x.shape
    tm = 256
    idx = pl.BlockSpec((tm,), lambda i: (i,), memory_space=pltpu.SMEM)
    return pl.pallas_call(
        _combine_kernel,
        grid=(T // tm,),
        in_specs=[idx, idx,
                  pl.BlockSpec((tm, LANES), lambda i: (i, 0)),
                  pl.BlockSpec((tm, D), lambda i: (i, 0)),
                  _mod_spec(5, tm, S),
                  pl.BlockSpec(memory_space=pl.ANY)],
        out_specs=pl.BlockSpec((tm, D), lambda i: (i, 0)),
        out_shape=jax.ShapeDtypeStruct((T, D), F32),
        scratch_shapes=[pltpu.VMEM((tm, D), F32), pltpu.VMEM((tm, D), F32),
                        pltpu.SemaphoreType.DMA(()), pltpu.SemaphoreType.DMA(())],
        compiler_params=_cparams("arbitrary"),
        name="moe_combine",
    )(pos1, pos2, sel, x, mod, ys)


def _moe(x, g, mod, wr_pad, wg, wu, wd, S):
    T = x.shape[0]
    n_tiles = (2 * T) // MOE_TM + N_EXPERTS
    hn, sel, cnt = _router(x, g, mod, wr_pad, S)
    pos1, pos2, src, expert_of, valid = _route_tables(sel, cnt, T, n_tiles)
    ys = _moe_ffn(hn, src, expert_of, valid, wg, wu, wd)
    return _combine(ys, pos1, pos2, sel, x, mod, S)


def _cast_kernel(w_ref, o_ref):
    o_ref[...] = w_ref[...].astype(BF16)


def _cast_layer(w, l):
    if w.ndim == 3:
        return _cast_layer(w[:, None], l)[0]
    _, E, R, C = w.shape
    tr = 256
    return pl.pallas_call(
        _cast_kernel,
        grid=(E, R // tr),
        in_specs=[pl.BlockSpec((None, None, tr, C), lambda e, r: (l, e, r, 0))],
        out_specs=pl.BlockSpec((None, tr, C), lambda e, r: (e, r, 0)),
        out_shape=jax.ShapeDtypeStruct((E, R, C), BF16),
        compiler_params=_cparams("parallel", "parallel"),
        name="cast_bf16",
    )(w)


def _final_kernel(x_ref, g_ref, o_ref):
    x = x_ref[...]
    o_ref[...] = x * lax.rsqrt(jnp.mean(x * x, axis=-1, keepdims=True) + NORM_EPS) * g_ref[...]


def _final_norm(x, g):
    T, D = x.shape
    tm = 1024
    return pl.pallas_call(
        _final_kernel,
        grid=(T // tm,),
        in_specs=[pl.BlockSpec((tm, D), lambda i: (i, 0)), pl.BlockSpec((1, D), lambda i: (0, 0))],
        out_specs=pl.BlockSpec((tm, D), lambda i: (i, 0)),
        out_shape=jax.ShapeDtypeStruct((T, D), F32),
        compiler_params=_cparams("parallel"),
        name="final_norm",
    )(x, g.reshape(1, D))


def _prep_layer(l, p):
    t5 = p["t5_table"]
    sw_tab = jnp.stack([t5[:, h] for h in SW_PERM])
    dil_tab = t5[:, SW_HEADS:].reshape(T5_BUCKETS, DIL_GROUPS, DIL_HEADS).transpose(1, 2, 0)
    w_in = p["w_in"][l]
    out = {
        "w_main": _take(w_in, _MAIN_RUNS, 1),
        "w_dil": [_take(w_in, _DIL_RUNS[g], 1) for g in range(DIL_GROUPS)],
        "w_uq": _take(p["mla_w_uq"][l], _UQ_RUNS, 1),
        "w_ukv": _take(p["mla_w_ukv"][l], _UKV_RUNS, 1),
        "w_out": _take(p["w_out"][l], _OUT_RUNS, 0),
        "na_bias": _na_bias(p["na_rpb"][l]),
        "sw_bias": _band_bias(sw_tab, SW_RADIUS, 1, LANES, LANES + 2 * SW_RADIUS),
        "sinks": jnp.stack([p["sw_sinks"][l][h] for h in SW_PERM]) * LOG2E,
        "dil_tab": dil_tab,
    }
    if l % 2 == 0:
        out["wg"] = _cast_layer(p["ffn_w_gate"], l // 2)
        out["wu"] = _cast_layer(p["ffn_w_up"], l // 2)
        out["wd"] = _cast_layer(p["ffn_w_down"], l // 2)
    else:
        out["wr"] = jnp.pad(p["moe_w_router"][l // 2], ((0, 0), (0, LANES - N_EXPERTS)))
        out["wg"] = _cast_layer(p["moe_w_gate"], l // 2)
        out["wu"] = _cast_layer(p["moe_w_up"], l // 2)
        out["wd"] = _cast_layer(p["moe_w_down"], l // 2)
    return out


def _token_mix(proj, hn, lw, p, l, B, S, rope):
    o_a = _na_attn(proj, lw["na_bias"], B, S)
    qm, km, vm = _mla_up(proj, rope[0], rope[1], p["mla_g_q"][l], p["mla_g_kv"][l], lw["w_uq"], lw["w_ukv"], S)
    o_b = _mla_attn(qm, km, vm, B, S)
    o_c = _sw_attn(proj, lw["sw_bias"], lw["sinks"], B, S)
    outs = []
    for g, dil in enumerate(DIL_DILATIONS):
        L = S // dil
        bias = _band_bias(lw["dil_tab"][g], DIL_SIDE, dil, LANES, min(L, LANES + 2 * DIL_SIDE))
        outs.append(_dil_attn(_dil_proj(hn, lw["w_dil"][g], B, S, dil), bias, dil))
    o_d = _merge(outs, B, S)
    return o_a, o_b, o_c, o_d


def _run_trunk(x, mods, layer_w, p, B, S):
    T = B * S
    x = x.reshape(T, D_MODEL)
    rope = _rope_tables(S)
    for l in range(DEPTH):
        lw = layer_w[l]
        mod = mods[l]
        proj, hn = _in_proj(x, p["g_mix"][l], mod, lw["w_main"], S)
        o_a, o_b, o_c, o_d = _token_mix(proj, hn, lw, p, l, B, S, rope)
        x = _out_proj(o_a, o_b, o_c, o_d, lw["w_out"], x, mod, S)
        if l % 2 == 0:
            x = _ffn(x, p["g_ffn"][l], mod, lw["wg"], lw["wu"], lw["wd"], S)
        else:
            x = _moe(x, p["g_ffn"][l], mod, lw["wr"], lw["wg"], lw["wu"], lw["wd"], S)
    return _final_norm(x, p["g_final"]).reshape(B, S, D_MODEL)


def kernel(x_prompt, x_sample, c_prompt, c_sample, w_ada, b_ada, g_mix, g_ffn, w_in, mla_g_q, mla_g_kv, mla_w_uq, mla_w_ukv, na_rpb, sw_sinks, t5_table, w_out, ffn_w_gate, ffn_w_up, ffn_w_down, moe_w_router, moe_w_gate, moe_w_up, moe_w_down, g_final):
    p = dict(g_mix=g_mix, g_ffn=g_ffn, w_in=w_in, mla_g_q=mla_g_q, mla_g_kv=mla_g_kv, mla_w_uq=mla_w_uq,
             mla_w_ukv=mla_w_ukv, na_rpb=na_rpb, sw_sinks=sw_sinks, t5_table=t5_table, w_out=w_out,
             ffn_w_gate=ffn_w_gate, ffn_w_up=ffn_w_up, ffn_w_down=ffn_w_down, moe_w_router=moe_w_router,
             moe_w_gate=moe_w_gate, moe_w_up=moe_w_up, moe_w_down=moe_w_down, g_final=g_final)
    Bp, Sp, _ = x_prompt.shape
    Bs, Ss, _ = x_sample.shape
    rows = 16
    c_pad = jnp.concatenate([c_prompt, c_sample, jnp.zeros((rows - Bp - Bs, D_MODEL), F32)], axis=0)
    mod_all = _ada(c_pad, w_ada, b_ada)
    mods_p = [mod_all[l, :Bp].reshape(Bp, 6, 1, D_MODEL) for l in range(DEPTH)]
    mods_s = [mod_all[l, Bp:Bp + Bs].reshape(Bs, 6, 1, D_MODEL) for l in range(DEPTH)]
    layer_w = [_prep_layer(l, p) for l in range(DEPTH)]
    y_prompt = _run_trunk(x_prompt, mods_p, layer_w, p, Bp, Sp)
    y_sample = _run_trunk(x_sample, mods_s, layer_w, p, Bs, Ss)
    return (y_prompt, y_sample)
```

```python
import functools
import math

import numpy as np
import jax
import jax.numpy as jnp
from jax import lax
from jax.experimental import pallas as pl
from jax.experimental.pallas import tpu as pltpu

F32 = jnp.float32
BF16 = jnp.bfloat16

D_MODEL = 2048
DEPTH = 4
HEAD_DIM = 64
GRID_W = 64
NA_HEADS = 8
NA_ROWS = 8
NA_COLS = 16
MLA_HEADS = 8
MLA_Q_RANK = 512
MLA_KV_RANK = 256
MLA_NOPE = 64
MLA_ROPE = 32
MLA_V = 64
ROPE_THETA = 10000.0
SW_HEADS = 8
SW_KV_HEADS = 2
SW_RADIUS = 128
DIL_DILATIONS = (1, 4, 16)
DIL_GROUPS = 3
DIL_HEADS = 8
DIL_SIDE = 64
T5_BUCKETS = 32
T5_MAX_DIST = 1024
N_EXPERTS = 8
D_FF = 5632
NORM_EPS = 1e-6

LANES = 128
NEG = -1e30
LOG2E = math.log2(math.e)
VMEM_LIMIT = 56 * 1024 * 1024
MOE_TM = 512
MLA_SUM_ROWS = 16

NA_IN = 3 * NA_HEADS * HEAD_DIM
MLA_IN = MLA_Q_RANK + MLA_KV_RANK + MLA_ROPE
SW_IN = (SW_HEADS + 2 * SW_KV_HEADS) * HEAD_DIM
DIL_IN = 3 * DIL_GROUPS * DIL_HEADS * HEAD_DIM
P_NA = 0
P_CQ = 1536
P_CKV = 2048
P_KR = 2304
P_KRR = 2432
P_SWQ = 2560
P_SWK = 3072
P_SWV = 3200
P_MAIN = 3584
GW = DIL_HEADS * HEAD_DIM
SW_PERM = (0, 4, 1, 5, 2, 6, 3, 7)


def _cparams(*sem):
    return pltpu.CompilerParams(dimension_semantics=sem, vmem_limit_bytes=VMEM_LIMIT)


def _nt_dot(a, b):
    return lax.dot_general(a, b, (((1,), (1,)), ((), ())), preferred_element_type=F32)


def _dot(a, b):
    return jnp.dot(a, b, preferred_element_type=F32)


def _rot_half_cols(base):
    half = MLA_ROPE // 2
    src = [base + half + j for j in range(half)] + [base + j for j in range(half)]
    sgn = [-1.0] * half + [1.0] * half
    return src, sgn


def _main_colmap():
    src = np.zeros((P_MAIN,), np.int32)
    mul = np.zeros((P_MAIN,), np.float32)
    qs = HEAD_DIM ** -0.5 * LOG2E
    for j in range(NA_IN):
        src[P_NA + j] = j
        mul[P_NA + j] = qs if j < NA_HEADS * HEAD_DIM else 1.0
    b0 = NA_IN
    for j in range(MLA_Q_RANK + MLA_KV_RANK):
        src[P_CQ + j] = b0 + j
        mul[P_CQ + j] = 1.0
    kr0 = b0 + MLA_Q_RANK + MLA_KV_RANK
    rsrc, rsgn = _rot_half_cols(kr0)
    for rep in range(2):
        for j in range(MLA_ROPE):
            src[P_KR + rep * MLA_ROPE + j] = kr0 + j
            mul[P_KR + rep * MLA_ROPE + j] = 1.0
            src[P_KRR + rep * MLA_ROPE + j] = rsrc[j]
            mul[P_KRR + rep * MLA_ROPE + j] = rsgn[j]
    c0 = NA_IN + MLA_IN
    for hh, h in enumerate(SW_PERM):
        for d in range(HEAD_DIM):
            src[P_SWQ + hh * HEAD_DIM + d] = c0 + h * HEAD_DIM + d
            mul[P_SWQ + hh * HEAD_DIM + d] = qs
    for j in range(2 * SW_KV_HEADS * HEAD_DIM):
        src[P_SWK + j] = c0 + SW_HEADS * HEAD_DIM + j
        mul[P_SWK + j] = 1.0
    return src, mul


def _dil_colmap(g):
    d0 = NA_IN + MLA_IN + SW_IN
    src = np.zeros((3 * GW,), np.int32)
    mul = np.ones((3 * GW,), np.float32)
    for t in range(3):
        for j in range(GW):
            src[t * GW + j] = d0 + (t * DIL_GROUPS + g) * GW + j
    mul[:GW] = HEAD_DIM ** -0.5 * LOG2E
    return src, mul


def _mla_q_colmap():
    n = 3 * 4 * LANES
    src = np.zeros((n,), np.int32)
    mul = np.zeros((n,), np.float32)
    hw = MLA_NOPE + MLA_ROPE
    for h in range(MLA_HEADS):
        p, a = divmod(h, 2)
        for d in range(MLA_NOPE):
            src[p * LANES + a * MLA_NOPE + d] = h * hw + d
            mul[p * LANES + a * MLA_NOPE + d] = 1.0
        rsrc, rsgn = _rot_half_cols(h * hw + MLA_NOPE)
        for j in range(MLA_ROPE):
            ca = 4 * LANES + p * LANES + a * MLA_ROPE + j
            cb = 8 * LANES + p * LANES + a * MLA_ROPE + j
            src[ca] = h * hw + MLA_NOPE + j
            mul[ca] = 1.0
            src[cb] = rsrc[j]
            mul[cb] = rsgn[j]
    return src, mul


def _mla_kv_colmap():
    n = 2 * 4 * LANES
    src = np.zeros((n,), np.int32)
    hw = MLA_NOPE + MLA_V
    for h in range(MLA_HEADS):
        for d in range(MLA_NOPE):
            src[h * MLA_NOPE + d] = h * hw + d
            src[4 * LANES + h * MLA_V + d] = h * hw + MLA_NOPE + d
    return src, np.ones((n,), np.float32)


def _out_proj_rowmap():
    src = np.arange(4 * 512, dtype=np.int32)
    for hh, h in enumerate(SW_PERM):
        for d in range(HEAD_DIM):
            src[1024 + hh * HEAD_DIM + d] = 1024 + h * HEAD_DIM + d
    return src, np.ones((4 * 512,), np.float32)


def _runs(colmap):
    src, mul = colmap
    runs, i, n = [], 0, len(src)
    while i < n:
        j = i + 1
        if mul[i] == 0.0:
            while j < n and mul[j] == 0.0:
                j += 1
            runs.append((None, j - i, 0.0))
        else:
            while j < n and mul[j] == mul[i] and src[j] == src[j - 1] + 1:
                j += 1
            runs.append((int(src[i]), j - i, float(mul[i])))
        i = j
    return runs


_MAIN_RUNS = _runs(_main_colmap())
_DIL_RUNS = [_runs(_dil_colmap(g)) for g in range(DIL_GROUPS)]
_UQ_RUNS = _runs(_mla_q_colmap())
_UKV_RUNS = _runs(_mla_kv_colmap())
_OUT_RUNS = _runs(_out_proj_rowmap())


def _take(w, runs, axis):
    parts = []
    for start, n, m in runs:
        if start is None:
            shape = list(w.shape)
            shape[axis] = n
            parts.append(jnp.zeros(shape, w.dtype))
        else:
            piece = lax.slice_in_dim(w, start, start + n, axis=axis)
            parts.append(piece if m == 1.0 else piece * m)
    return jnp.concatenate(parts, axis=axis).astype(BF16)


def _one_hot(idx, n):
    return (idx[..., None] == jnp.arange(n, dtype=idx.dtype)).astype(F32)


def _select(one_hot, table, spec):
    return jnp.einsum(spec, one_hot, table, precision=lax.Precision.HIGHEST, preferred_element_type=F32)


def _t5_bucket(rel):
    half = T5_BUCKETS // 2
    max_exact = half // 2
    side = jnp.where(rel > 0, half, 0)
    n = jnp.abs(rel)
    nf = jnp.maximum(n, 1).astype(F32)
    large = max_exact + (jnp.log(nf / max_exact) / math.log(T5_MAX_DIST / max_exact) * (half - max_exact)).astype(jnp.int32)
    large = jnp.minimum(large, half - 1)
    return side + jnp.where(n < max_exact, n, large)


def _pair_rows(b):
    H, V, q, k = b.shape
    return b.reshape(H // 2, 2, V, q, k).transpose(0, 2, 1, 3, 4).reshape(H // 2, V, 2 * q, k)


def _band_bias(table_hb, radius, dilation, tq, kw):
    qa = np.arange(tq)[:, None]
    kc = np.arange(kw)[None, :]
    outs = []
    for v in range(3):
        rel = kc - qa - v * radius
        valid = jnp.asarray(np.abs(rel) <= radius)
        bucket = _t5_bucket(jnp.asarray(rel * dilation, jnp.int32))
        b = _select(_one_hot(bucket, T5_BUCKETS), table_hb, "qkb,hb->hqk")
        outs.append(jnp.where(valid[None], b * LOG2E, NEG))
    return _pair_rows(jnp.stack(outs, axis=1).astype(F32))


def _na_bias(rpb):
    c = np.arange(GRID_W)[:, None]
    kc = np.arange(GRID_W)[None, :]
    col_start = np.clip(c - NA_COLS // 2, 0, GRID_W - NA_COLS)
    valid = (kc >= col_start) & (kc < col_start + NA_COLS)
    cidx = np.clip(kc - c + NA_COLS - 1, 0, 2 * NA_COLS - 2)
    d = np.arange(NA_ROWS)[:, None]
    kr = np.arange(NA_ROWS)[None, :]
    ridx = kr - d + NA_ROWS - 1
    rsel = _one_hot(jnp.asarray(ridx.reshape(-1)), 2 * NA_ROWS - 1)
    csel = _one_hot(jnp.asarray(cidx.reshape(-1)), 2 * NA_COLS - 1)
    b = _select(rsel, _select(csel, rpb, "cj,hij->hic"), "ri,hic->hrc")
    b = b.reshape(NA_HEADS, NA_ROWS, NA_ROWS, GRID_W, GRID_W)
    b = jnp.where(jnp.asarray(valid)[None, None, None], b * LOG2E, NEG)
    b = b.transpose(0, 1, 3, 2, 4)
    return _pair_rows(b.reshape(NA_HEADS, NA_ROWS, GRID_W, NA_ROWS * GRID_W).astype(F32))


def _rope_tables(S):
    inv = jnp.power(jnp.float32(ROPE_THETA), -jnp.arange(0, MLA_ROPE, 2, dtype=F32) / MLA_ROPE)
    ang = jnp.arange(S, dtype=F32)[:, None] * inv[None, :]
    cos, sin = jnp.cos(ang), jnp.sin(ang)
    z = jnp.zeros((S, LANES - 2 * MLA_ROPE), F32)
    cos_l = jnp.concatenate([cos, cos, cos, cos, z], axis=1)
    sin_l = jnp.concatenate([sin, sin, sin, sin, z], axis=1)
    return cos_l, sin_l


def _ada_kernel(c_ref, w_ref, b_ref, o_ref):
    c = c_ref[...]
    cs = (c * jax.nn.sigmoid(c)).astype(BF16)
    o_ref[...] = _dot(cs, w_ref[...].astype(BF16)) + b_ref[...]


def _ada(c_pad, w_ada, b_ada):
    L, D, N = w_ada.shape
    R = c_pad.shape[0]
    tn = 1024
    return pl.pallas_call(
        _ada_kernel,
        grid=(L, N // tn),
        in_specs=[
            pl.BlockSpec((R, D), lambda l, j: (0, 0)),
            pl.BlockSpec((None, D, tn), lambda l, j: (l, 0, j)),
            pl.BlockSpec((None, 1, tn), lambda l, j: (l, 0, j)),
        ],
        out_specs=pl.BlockSpec((None, R, tn), lambda l, j: (l, 0, j)),
        out_shape=jax.ShapeDtypeStruct((L, R, N), F32),
        compiler_params=_cparams("parallel", "parallel"),
        name="ada",
    )(c_pad, w_ada, b_ada.reshape(L, 1, N))


def _norm_mod(x, g, sh, sc):
    y = x * lax.rsqrt(jnp.mean(x * x, axis=-1, keepdims=True) + NORM_EPS) * g
    return y * (1.0 + sc) + sh


def _mod_spec(k, tm, S):
    return pl.BlockSpec((None, None, 1, D_MODEL), lambda i, *_: ((i * tm) // S, k, 0, 0))


def _lane_half_mask(shape):
    return lax.broadcasted_iota(jnp.int32, shape, len(shape) - 1) < HEAD_DIM


def _stack_heads(q, first):
    zero = jnp.zeros_like(q)
    return jnp.concatenate([jnp.where(first, q, zero), jnp.where(first, zero, q)], axis=0)


def _in_kernel(x_ref, g_ref, sh_ref, sc_ref, w_ref, o_ref, hn_ref):
    @pl.when(pl.program_id(1) == 0)
    def _():
        hn_ref[...] = _norm_mod(x_ref[...], g_ref[...], sh_ref[...], sc_ref[...]).astype(BF16)

    o_ref[...] = _dot(hn_ref[...], w_ref[...]).astype(BF16)


def _in_proj(x, g, mod, w, S):
    T, D = x.shape
    N = w.shape[1]
    tm, tn = 512, N // 2
    return pl.pallas_call(
        _in_kernel,
        grid=(T // tm, N // tn),
        in_specs=[
            pl.BlockSpec((tm, D), lambda i, j: (i, 0)),
            pl.BlockSpec((1, D), lambda i, j: (0, 0)),
            _mod_spec(0, tm, S),
            _mod_spec(1, tm, S),
            pl.BlockSpec((D, tn), lambda i, j: (0, j)),
        ],
        out_specs=[pl.BlockSpec((tm, tn), lambda i, j: (i, j)),
                   pl.BlockSpec((tm, D), lambda i, j: (i, 0))],
        out_shape=[jax.ShapeDtypeStruct((T, N), BF16), jax.ShapeDtypeStruct((T, D), BF16)],
        compiler_params=_cparams("parallel", "arbitrary"),
        name="in_proj",
    )(x, g.reshape(1, D), mod, mod, w)


def _dil_proj_kernel(h_ref, w_ref, o_ref, *scratch, dil):
    r = _dot(h_ref[...], w_ref[...])
    if dil == 1:
        o_ref[0] = r.astype(BF16)
    else:
        r_ref, = scratch
        n = r_ref.shape[1] // dil
        for jb in range(r_ref.shape[0]):
            cs = slice(jb * LANES, (jb + 1) * LANES)
            r_ref[jb] = r[:, cs]
            for c in range(dil):
                o_ref[c, :, cs] = r_ref[jb, pl.ds(c, n, stride=dil), :].astype(BF16)


def _dil_proj(hn, w, B, S, dil):
    T, D = hn.shape
    N = w.shape[1]
    tm = 1024
    nsb = S // tm
    scratch = [] if dil == 1 else [pltpu.VMEM((N // LANES, tm, LANES), F32)]
    return pl.pallas_call(
        functools.partial(_dil_proj_kernel, dil=dil),
        grid=(T // tm,),
        in_specs=[pl.BlockSpec((tm, D), lambda i: (i, 0)), pl.BlockSpec((D, N), lambda i: (0, 0))],
        out_specs=pl.BlockSpec((None, dil, tm // dil, N), lambda i: (i // nsb, 0, i % nsb, 0)),
        out_shape=jax.ShapeDtypeStruct((B, dil, S // dil, N), BF16),
        scratch_shapes=scratch,
        compiler_params=_cparams("parallel"),
        name="dil_proj_d%d" % dil,
    )(hn, w)


def _softmax_pv(s, v):
    m = jnp.max(s, axis=-1, keepdims=True)
    p = jnp.exp2(s - m)
    l = jnp.sum(p, axis=-1, keepdims=True)
    return _dot(p.astype(BF16), v) / l


def _issue_ahead(n_items, scores, finish):
    cur = scores(0)
    for i in range(n_items):
        nxt = scores(i + 1) if i + 1 < n_items else None
        finish(i, cur)
        cur = nxt


def _na_kernel(q_ref, k_ref, v_ref, bias_ref, o_ref, *, rows, qrows):
    rb = pl.program_id(1)
    kwin = NA_ROWS * GRID_W
    n_pairs = NA_HEADS // 2
    lo = _lane_half_mask((GRID_W, LANES))

    def window(qr):
        r = rb * qrows + qr
        r_start = jnp.clip(r - NA_ROWS // 2, 0, rows - NA_ROWS)
        return r - r_start, pl.multiple_of(r_start * GRID_W, GRID_W)

    def scores(qr):
        d, k0 = window(qr)
        out = []
        for p in range(n_pairs):
            cs = slice(p * LANES, (p + 1) * LANES)
            q = q_ref[qr * GRID_W:(qr + 1) * GRID_W, cs]
            out.append(_nt_dot(_stack_heads(q, lo), k_ref[pl.ds(k0, kwin), cs]) + bias_ref[p, d])
        return out

    def finish(qr, ss):
        _, k0 = window(qr)
        for p in range(n_pairs):
            cs = slice(p * LANES, (p + 1) * LANES)
            o = _softmax_pv(ss[p], v_ref[pl.ds(k0, kwin), cs])
            o_ref[qr * GRID_W:(qr + 1) * GRID_W, cs] = jnp.where(lo, o[:GRID_W], o[GRID_W:]).astype(BF16)

    _issue_ahead(qrows, scores, finish)


def _na_attn(proj, bias, B, S):
    T = B * S
    rows = S // GRID_W
    qrows = 8
    tq = qrows * GRID_W
    nqb = S // tq
    w = NA_HEADS * HEAD_DIM
    kern = functools.partial(_na_kernel, rows=rows, qrows=qrows)
    return pl.pallas_call(
        kern,
        grid=(B, nqb),
        in_specs=[
            pl.BlockSpec((tq, w), lambda b, r: (b * nqb + r, P_NA // w)),
            pl.BlockSpec((S, w), lambda b, r: (b, P_NA // w + 1)),
            pl.BlockSpec((S, w), lambda b, r: (b, P_NA // w + 2)),
            pl.BlockSpec(bias.shape, lambda b, r: (0, 0, 0, 0)),
        ],
        out_specs=pl.BlockSpec((tq, w), lambda b, r: (b * nqb + r, 0)),
        out_shape=jax.ShapeDtypeStruct((T, w), BF16),
        compiler_params=_cparams("parallel", "arbitrary"),
        name="na_attn",
    )(proj, proj, proj, bias)


def _mla_up_kernel(cq_ref, ckv_ref, kr_ref, krr_ref, cos_ref, sin_ref, gq_ref, gkv_ref, wq_ref, wkv_ref,
                   q_ref, k_ref, v_ref):
    scale = (MLA_NOPE + MLA_ROPE) ** -0.5 * LOG2E
    cos = cos_ref[...]
    sin = sin_ref[...]

    def rms(x_ref, g_ref):
        x = x_ref[...].astype(F32)
        return (x * lax.rsqrt(jnp.mean(x * x, axis=-1, keepdims=True) + NORM_EPS) * g_ref[...]).astype(BF16)

    qf = _dot(rms(cq_ref, gq_ref), wq_ref[...]) * scale
    kvf = _dot(rms(ckv_ref, gkv_ref), wkv_ref[...])
    kpe = (kr_ref[...].astype(F32) * cos + krr_ref[...].astype(F32) * sin).astype(BF16)
    for p in range(MLA_HEADS // 2):
        nope = qf[:, p * LANES:(p + 1) * LANES]
        pe = qf[:, (4 + p) * LANES:(5 + p) * LANES] * cos + qf[:, (8 + p) * LANES:(9 + p) * LANES] * sin
        q_ref[:, 2 * p * LANES:(2 * p + 1) * LANES] = nope.astype(BF16)
        q_ref[:, (2 * p + 1) * LANES:(2 * p + 2) * LANES] = pe.astype(BF16)
        k_ref[:, 2 * p * LANES:(2 * p + 1) * LANES] = kvf[:, p * LANES:(p + 1) * LANES].astype(BF16)
        k_ref[:, (2 * p + 1) * LANES:(2 * p + 2) * LANES] = kpe
    v_ref[...] = kvf[:, 4 * LANES:].astype(BF16)


def _mla_up(proj, cos_l, sin_l, g_q, g_kv, wq, wkv, S):
    T = proj.shape[0]
    tm = 512
    nsb = S // tm
    row = lambda i: (i, 0)
    return pl.pallas_call(
        _mla_up_kernel,
        grid=(T // tm,),
        in_specs=[
            pl.BlockSpec((tm, MLA_Q_RANK), lambda i: (i, P_CQ // MLA_Q_RANK)),
            pl.BlockSpec((tm, MLA_KV_RANK), lambda i: (i, P_CKV // MLA_KV_RANK)),
            pl.BlockSpec((tm, LANES), lambda i: (i, P_KR // LANES)),
            pl.BlockSpec((tm, LANES), lambda i: (i, P_KRR // LANES)),
            pl.BlockSpec((tm, LANES), lambda i: (i % nsb, 0)),
            pl.BlockSpec((tm, LANES), lambda i: (i % nsb, 0)),
            pl.BlockSpec((1, MLA_Q_RANK), lambda i: (0, 0)),
            pl.BlockSpec((1, MLA_KV_RANK), lambda i: (0, 0)),
            pl.BlockSpec(wq.shape, lambda i: (0, 0)),
            pl.BlockSpec(wkv.shape, lambda i: (0, 0)),
        ],
        out_specs=[
            pl.BlockSpec((tm, 8 * LANES), row),
            pl.BlockSpec((tm, 8 * LANES), row),
            pl.BlockSpec((tm, 4 * LANES), row),
        ],
        out_shape=[
            jax.ShapeDtypeStruct((T, 8 * LANES), BF16),
            jax.ShapeDtypeStruct((T, 8 * LANES), BF16),
            jax.ShapeDtypeStruct((T, 4 * LANES), BF16),
        ],
        compiler_params=_cparams("parallel"),
        name="mla_up",
    )(proj, proj, proj, proj, cos_l, sin_l, g_q.reshape(1, -1), g_kv.reshape(1, -1), wq, wkv)


def _mla_attn_kernel(q_ref, k_ref, v_ref, o_ref, vt_ref, *, tk):
    tq = q_ref.shape[0]
    nk = vt_ref.shape[0]

    @pl.when(pl.program_id(2) == 0)
    def _():
        for c in range(nk):
            vt_ref[c, :LANES, :] = v_ref[c * tk:(c + 1) * tk, :].astype(F32).T.astype(BF16)
            vt_ref[c, LANES:, :] = jnp.ones((MLA_SUM_ROWS, tk), BF16)

    q = q_ref[...]
    lane = lax.broadcasted_iota(jnp.int32, q.shape, 1)
    first = (lane < MLA_NOPE) | ((lane >= LANES) & (lane < LANES + MLA_ROPE))
    qcat = _stack_heads(q, first)

    def scores(c):
        return _nt_dot(k_ref[c * tk:(c + 1) * tk, :], qcat)

    m = jnp.full((1, 2 * tq), NEG, F32)
    acc = jnp.zeros((LANES + MLA_SUM_ROWS, 2 * tq), F32)
    st = scores(0)
    for c in range(nk):
        st_next = scores(c + 1) if c + 1 < nk else None
        mn = jnp.maximum(m, jnp.max(st, axis=0, keepdims=True))
        p = jnp.exp2(st - mn)
        acc = jnp.exp2(m - mn) * acc + _dot(vt_ref[c], p.astype(BF16))
        m, st = mn, st_next
    o = (acc[:LANES] / acc[LANES:LANES + 1]).T
    lo = _lane_half_mask((tq, LANES))
    o_ref[...] = jnp.where(lo, o[:tq], o[tq:]).astype(BF16)


def _mla_attn(qm, km, vm, B, S):
    T = B * S
    tq, tk = 512, 512
    nqb = S // tq
    kern = functools.partial(_mla_attn_kernel, tk=tk)
    return pl.pallas_call(
        kern,
        grid=(B, MLA_HEADS // 2, nqb),
        in_specs=[
            pl.BlockSpec((tq, 2 * LANES), lambda b, p, i: (b * nqb + i, p)),
            pl.BlockSpec((S, 2 * LANES), lambda b, p, i: (b, p)),
            pl.BlockSpec((S, LANES), lambda b, p, i: (b, p)),
        ],
        out_specs=pl.BlockSpec((tq, LANES), lambda b, p, i: (b * nqb + i, p)),
        out_shape=jax.ShapeDtypeStruct((T, MLA_HEADS * MLA_V), BF16),
        scratch_shapes=[pltpu.VMEM((S // tk, LANES + MLA_SUM_ROWS, tk), BF16)],
        compiler_params=_cparams("parallel", "parallel", "arbitrary"),
        name="mla_attn",
    )(qm, km, vm)


def _band_kernel(*refs, L, radius, tq, kw, n_pairs, shared_kv, with_sink, with_lse):
    if with_sink:
        sink_ref, refs = refs[0], refs[1:]
    q_ref, k_ref, v_ref, bias_ref = refs[:4]
    o_ref = refs[4]
    lse_ref = refs[5] if with_lse else None
    qb = pl.program_id(2)
    nq = q_ref.shape[0] // tq
    lo = _lane_half_mask((tq, LANES))
    second = lax.broadcasted_iota(jnp.int32, (2 * tq, 1), 0) >= tq

    def window(t):
        q0 = (qb * nq + t) * tq
        start = jnp.clip(q0 - radius, 0, L - kw)
        return (q0 - start) // radius, pl.multiple_of(start, min(radius, tq))

    def scores(t):
        var, k0 = window(t)
        out = []
        for p in range(n_pairs):
            cs = slice(p * LANES, (p + 1) * LANES)
            kcs = slice(0, LANES) if shared_kv else cs
            q = q_ref[t * tq:(t + 1) * tq, cs]
            out.append(_nt_dot(_stack_heads(q, lo), k_ref[pl.ds(k0, kw), kcs]) + bias_ref[p, var])
        return out

    def finish(t, ss):
        _, k0 = window(t)
        rs = slice(t * tq, (t + 1) * tq)
        for p in range(n_pairs):
            cs = slice(p * LANES, (p + 1) * LANES)
            kcs = slice(0, LANES) if shared_kv else cs
            s = ss[p]
            m = jnp.max(s, axis=-1, keepdims=True)
            if with_sink:
                sk = jnp.where(second, sink_ref[2 * p + 1], sink_ref[2 * p])
                m = jnp.maximum(m, sk)
            e = jnp.exp2(s - m)
            l = jnp.sum(e, axis=-1, keepdims=True)
            if with_sink:
                l = l + jnp.exp2(sk - m)
            o = _dot(e.astype(BF16), v_ref[pl.ds(k0, kw), kcs]) / l
            if with_lse:
                lse = jnp.broadcast_to(m + jnp.log2(l), (2 * tq, LANES))
                o_ref[rs, cs] = jnp.where(lo, o[:tq], o[tq:])
                lse_ref[rs, cs] = jnp.where(lo, lse[:tq], lse[tq:])
            else:
                o_ref[rs, cs] = jnp.where(lo, o[:tq], o[tq:]).astype(BF16)

    _issue_ahead(nq, scores, finish)


def _sw_attn(proj, bias, sinks, B, S):
    T = proj.shape[0]
    tq, radius = LANES, SW_RADIUS
    kw = tq + 2 * radius
    qblk = 512
    nqb = S // qblk
    wq = SW_HEADS * HEAD_DIM
    kern = functools.partial(_band_kernel, L=S, radius=radius, tq=tq, kw=kw, n_pairs=wq // LANES,
                             shared_kv=True, with_sink=True, with_lse=False)
    return pl.pallas_call(
        kern,
        grid=(B, 1, nqb),
        in_specs=[
            pl.BlockSpec(memory_space=pltpu.SMEM),
            pl.BlockSpec((qblk, wq), lambda b, c, i: (b * nqb + i, P_SWQ // wq)),
            pl.BlockSpec((S, LANES), lambda b, c, i: (b, P_SWK // LANES)),
            pl.BlockSpec((S, LANES), lambda b, c, i: (b, P_SWV // LANES)),
            pl.BlockSpec(bias.shape, lambda b, c, i: (0, 0, 0, 0)),
        ],
        out_specs=pl.BlockSpec((qblk, wq), lambda b, c, i: (b * nqb + i, 0)),
        out_shape=jax.ShapeDtypeStruct((T, wq), BF16),
        compiler_params=_cparams("parallel", "parallel", "arbitrary"),
        name="band_attn_sw",
    )(sinks, proj, proj, proj, bias)


def _dil_attn(qkv, bias, dil):
    B, _, L, _ = qkv.shape
    tq, radius = LANES, DIL_SIDE
    kw = min(L, tq + 2 * radius)
    qblk = min(L, 512)
    nqb = L // qblk
    kern = functools.partial(_band_kernel, L=L, radius=radius, tq=tq, kw=kw, n_pairs=GW // LANES,
                             shared_kv=False, with_sink=False, with_lse=True)
    o_spec = pl.BlockSpec((None, None, qblk, GW), lambda b, c, i: (b, c, i, 0))
    o_shape = jax.ShapeDtypeStruct((B, dil, L, GW), F32)
    return pl.pallas_call(
        kern,
        grid=(B, dil, nqb),
        in_specs=[
            pl.BlockSpec((None, None, qblk, GW), lambda b, c, i: (b, c, i, 0)),
            pl.BlockSpec((None, None, L, GW), lambda b, c, i: (b, c, 0, 1)),
            pl.BlockSpec((None, None, L, GW), lambda b, c, i: (b, c, 0, 2)),
            pl.BlockSpec(bias.shape, lambda b, c, i: (0, 0, 0, 0)),
        ],
        out_specs=[o_spec, o_spec],
        out_shape=[o_shape, o_shape],
        compiler_params=_cparams("parallel", "parallel", "arbitrary"),
        name="band_attn_d%d" % dil,
    )(qkv, qkv, qkv, bias)


def _merge_kernel(o0, l0, o1, l1, o2, l2, o_ref, *scratch, dils):
    for jb in range(GW // LANES):
        cs = slice(jb * LANES, (jb + 1) * LANES)
        vals = []
        for g, (o_in, l_in) in enumerate(((o0, l0), (o1, l1), (o2, l2))):
            d = dils[g]
            if d == 1:
                vals.append((o_in[0, :, cs], l_in[0, :, cs]))
            else:
                so, sl = scratch[2 * (g - 1)], scratch[2 * (g - 1) + 1]
                n = so.shape[1] // d
                for c in range(d):
                    so[jb, pl.ds(c, n, stride=d), :] = o_in[c, :, cs]
                    sl[jb, pl.ds(c, n, stride=d), :] = l_in[c, :, cs]
                vals.append((so[jb], sl[jb]))
        (a, la), (b, lb), (c, lc) = vals
        m = jnp.maximum(jnp.maximum(la, lb), lc)
        wa, wb, wc = jnp.exp2(la - m), jnp.exp2(lb - m), jnp.exp2(lc - m)
        o_ref[:, cs] = ((wa * a + wb * b + wc * c) / (wa + wb + wc)).astype(BF16)


def _merge(outs, B, S):
    T = B * S
    tm = 1024
    nsb = S // tm
    in_specs, args = [], []
    for d, (o, l) in zip(DIL_DILATIONS, outs):
        spec = pl.BlockSpec((None, d, tm // d, GW), lambda i: (i // nsb, 0, i % nsb, 0))
        in_specs += [spec, spec]
        args += [o, l]
    scratch = [pltpu.VMEM((GW // LANES, tm, LANES), F32) for d in DIL_DILATIONS[1:] for _ in range(2)]
    return pl.pallas_call(
        functools.partial(_merge_kernel, dils=DIL_DILATIONS),
        grid=(T // tm,),
        in_specs=in_specs,
        out_specs=pl.BlockSpec((tm, GW), lambda i: (i, 0)),
        out_shape=jax.ShapeDtypeStruct((T, GW), BF16),
        scratch_shapes=scratch,
        compiler_params=_cparams("parallel"),
        name="dil_merge",
    )(*args)


def _out_kernel(oa_ref, ob_ref, oc_ref, od_ref, w_ref, x_ref, gt_ref, o_ref):
    acc = _dot(oa_ref[...], w_ref[0:512, :])
    acc += _dot(ob_ref[...], w_ref[512:1024, :])
    acc += _dot(oc_ref[...], w_ref[1024:1536, :])
    acc += _dot(od_ref[...], w_ref[1536:2048, :])
    o_ref[...] = x_ref[...] + gt_ref[...] * acc


def _out_proj(oa, ob, oc, od, w, x, mod, S):
    T, D = x.shape
    tm = 512
    mix = pl.BlockSpec((tm, 512), lambda i: (i, 0))
    return pl.pallas_call(
        _out_kernel,
        grid=(T // tm,),
        in_specs=[mix, mix, mix, mix,
                  pl.BlockSpec(w.shape, lambda i: (0, 0)),
                  pl.BlockSpec((tm, D), lambda i: (i, 0)),
                  _mod_spec(2, tm, S)],
        out_specs=pl.BlockSpec((tm, D), lambda i: (i, 0)),
        out_shape=jax.ShapeDtypeStruct((T, D), F32),
        compiler_params=_cparams("parallel"),
        name="out_proj",
    )(oa, ob, oc, od, w, x, mod)


def _swiglu_tile(h, wg, wu, wd):
    g = _dot(h, wg)
    u = _dot(h, wu)
    return _dot((g * jax.nn.sigmoid(g) * u).astype(BF16), wd)


def _ffn_kernel(x_ref, g_ref, sh_ref, sc_ref, gt_ref, wg_ref, wu_ref, wd_ref, o_ref, hn_ref, acc_ref):
    j = pl.program_id(1)

    @pl.when(j == 0)
    def _():
        hn_ref[...] = _norm_mod(x_ref[...], g_ref[...], sh_ref[...], sc_ref[...]).astype(BF16)
        acc_ref[...] = jnp.zeros_like(acc_ref)

    acc_ref[...] += _swiglu_tile(hn_ref[...], wg_ref[...], wu_ref[...], wd_ref[...])

    @pl.when(j == pl.num_programs(1) - 1)
    def _():
        o_ref[...] = x_ref[...] + gt_ref[...] * acc_ref[...]


def _ffn(x, g, mod, wg, wu, wd, S):
    T, D = x.shape
    F = wg.shape[1]
    tm, tf = 512, 512
    return pl.pallas_call(
        _ffn_kernel,
        grid=(T // tm, F // tf),
        in_specs=[
            pl.BlockSpec((tm, D), lambda i, j: (i, 0)),
            pl.BlockSpec((1, D), lambda i, j: (0, 0)),
            _mod_spec(3, tm, S),
            _mod_spec(4, tm, S),
            _mod_spec(5, tm, S),
            pl.BlockSpec((D, tf), lambda i, j: (0, j)),
            pl.BlockSpec((D, tf), lambda i, j: (0, j)),
            pl.BlockSpec((tf, D), lambda i, j: (j, 0)),
        ],
        out_specs=pl.BlockSpec((tm, D), lambda i, j: (i, 0)),
        out_shape=jax.ShapeDtypeStruct((T, D), F32),
        scratch_shapes=[pltpu.VMEM((tm, D), BF16), pltpu.VMEM((tm, D), F32)],
        compiler_params=_cparams("parallel", "arbitrary"),
        name="ffn",
    )(x, g.reshape(1, D), mod, mod, mod, wg, wu, wd)


SEL_I1, SEL_I2, SEL_G1, SEL_G2, SEL_R1, SEL_R2 = range(6)


def _lane_pick(x, lane, k):
    return jnp.sum(jnp.where(lane == k, x, 0.0), axis=-1, keepdims=True)


def _router_kernel(x_ref, g_ref, sh_ref, sc_ref, wr_ref, hn_ref, sel_ref, cnt_ref):
    @pl.when(pl.program_id(0) == 0)
    def _():
        cnt_ref[...] = jnp.zeros_like(cnt_ref)

    h = _norm_mod(x_ref[...], g_ref[...], sh_ref[...], sc_ref[...])
    hn_ref[...] = h
    tm = h.shape[0]
    logits = jnp.dot(h, wr_ref[...], preferred_element_type=F32, precision=lax.Precision.HIGHEST)
    lane = lax.broadcasted_iota(jnp.int32, logits.shape, 1)
    lanef = lane.astype(F32)
    logits = jnp.where(lane < N_EXPERTS, logits, NEG)
    m1 = jnp.max(logits, axis=-1, keepdims=True)
    i1 = jnp.min(jnp.where(logits == m1, lanef, float(LANES)), axis=-1, keepdims=True)
    rest = jnp.where(lanef == i1, NEG, logits)
    m2 = jnp.max(rest, axis=-1, keepdims=True)
    i2 = jnp.min(jnp.where(rest == m2, lanef, float(LANES)), axis=-1, keepdims=True)
    e2 = jnp.exp(m2 - m1)
    g1 = 1.0 / (1.0 + e2)
    g2 = e2 / (1.0 + e2)
    oh1 = jnp.where(lanef == i1, 1.0, 0.0)
    oh2 = jnp.where(lanef == i2, 1.0, 0.0)
    earlier = (lax.broadcasted_iota(jnp.int32, (tm, tm), 1) < lax.broadcasted_iota(jnp.int32, (tm, tm), 0))
    tri = jnp.where(earlier, 1.0, 0.0).astype(BF16)
    pre1 = _dot(tri, oh1.astype(BF16))
    pre2 = _dot(tri, oh2.astype(BF16))
    tot1 = jnp.sum(oh1, axis=0, keepdims=True)
    tot2 = jnp.sum(oh2, axis=0, keepdims=True)
    seen = cnt_ref[...]
    r1 = jnp.sum(oh1 * (pre1 + seen), axis=-1, keepdims=True)
    r2 = jnp.sum(oh2 * (pre2 + seen + tot1), axis=-1, keepdims=True)
    cnt_ref[...] = seen + tot1 + tot2
    sel = jnp.zeros(logits.shape, F32)
    for k, val in ((SEL_I1, i1), (SEL_I2, i2), (SEL_G1, g1), (SEL_G2, g2), (SEL_R1, r1), (SEL_R2, r2)):
        sel = jnp.where(lane == k, val, sel)
    sel_ref[...] = sel


def _router(x, g, mod, wr_pad, S):
    T, D = x.shape
    tm = 512
    return pl.pallas_call(
        _router_kernel,
        grid=(T // tm,),
        in_specs=[
            pl.BlockSpec((tm, D), lambda i: (i, 0)),
            pl.BlockSpec((1, D), lambda i: (0, 0)),
            _mod_spec(3, tm, S),
            _mod_spec(4, tm, S),
            pl.BlockSpec((D, LANES), lambda i: (0, 0)),
        ],
        out_specs=[pl.BlockSpec((tm, D), lambda i: (i, 0)),
                   pl.BlockSpec((tm, LANES), lambda i: (i, 0)),
                   pl.BlockSpec((1, LANES), lambda i: (0, 0))],
        out_shape=[jax.ShapeDtypeStruct((T, D), F32),
                   jax.ShapeDtypeStruct((T, LANES), F32),
                   jax.ShapeDtypeStruct((1, LANES), F32)],
        compiler_params=_cparams("arbitrary"),
        name="router",
    )(x, g.reshape(1, D), mod, mod, wr_pad)


def _route_tables(sel, cnt, T, n_tiles):
    i1 = sel[:, SEL_I1].astype(jnp.int32)
    i2 = sel[:, SEL_I2].astype(jnp.int32)
    r1 = sel[:, SEL_R1].astype(jnp.int32)
    r2 = sel[:, SEL_R2].astype(jnp.int32)
    counts = cnt[0, :N_EXPERTS].astype(jnp.int32)
    tiles_e = (counts + MOE_TM - 1) // MOE_TM
    ends = jnp.cumsum(tiles_e)
    offs = (ends - tiles_e) * MOE_TM
    pos1 = offs[i1] + r1
    pos2 = offs[i2] + r2
    tile = jnp.arange(n_tiles, dtype=jnp.int32)
    n_used = ends[-1]
    valid = (tile < n_used).astype(jnp.int32)
    expert_of = jnp.sum((tile[:, None] >= ends[None, :]).astype(jnp.int32), axis=1)
    expert_of = expert_of[jnp.minimum(tile, n_used - 1)]
    tok = jnp.arange(T, dtype=jnp.int32)
    src = jnp.zeros((n_tiles * MOE_TM,), jnp.int32).at[jnp.concatenate([pos1, pos2])].set(
        jnp.concatenate([tok, tok]), unique_indices=True)
    return pos1, pos2, src, expert_of, valid


def _row_copy(src_hbm, row, dst_ref, r, sem):
    return pltpu.make_async_copy(src_hbm.at[pl.ds(row, 1)], dst_ref.at[pl.ds(r, 1)], sem)


ROW_DMA_UNROLL = 8


def _start_rows(idx_ref, src_hbm, dst_ref, sem, first, count, alternate=True):
    def body(g, c):
        for u in range(ROW_DMA_UNROLL):
            r = first + g * ROW_DMA_UNROLL + u
            _row_copy(src_hbm, idx_ref[r], dst_ref, r, sem).start(priority=u % 2 if alternate else 0)
        return c

    lax.fori_loop(0, count // ROW_DMA_UNROLL, body, 0)


def _wait_rows(src_hbm, dst_ref, sem):
    def body(g, c):
        for u in range(ROW_DMA_UNROLL):
            _row_copy(src_hbm, 0, dst_ref, g * ROW_DMA_UNROLL + u, sem).wait()
        return c

    lax.fori_loop(0, dst_ref.shape[0] // ROW_DMA_UNROLL, body, 0)


def _moe_ffn_kernel(te_ref, va_ref, src0_ref, srcn_ref, h_hbm, wg_ref, wu_ref, wd_ref, o_ref,
                    xbuf_ref, hb_ref, sems, *, n_tiles, issue_steps):
    i = pl.program_id(0)
    j = pl.program_id(1)
    tm = hb_ref.shape[0]
    per_step = tm // issue_steps

    @pl.when((i == 0) & (j == 0))
    def _():
        _start_rows(src0_ref, h_hbm, xbuf_ref.at[0], sems.at[0], 0, tm, alternate=False)

    busy = va_ref[i] == 1
    fed = (i == 0) | (va_ref[jnp.maximum(i - 1, 0)] == 1)
    feeds = busy & (i + 1 < n_tiles) & (j < issue_steps)

    @pl.when((j == 0) & fed)
    def _():
        cur = i % 2
        _wait_rows(h_hbm, xbuf_ref.at[cur], sems.at[cur])
        hb_ref[...] = xbuf_ref[cur].astype(BF16)

    @pl.when(j == 0)
    def _():
        o_ref[...] = jnp.zeros_like(o_ref)

    def swiglu():
        o_ref[...] += _swiglu_tile(hb_ref[...], wg_ref[...], wu_ref[...], wd_ref[...])

    @pl.when(feeds)
    def _():
        nxt = (i + 1) % 2
        for u in range(per_step):
            r = j * per_step + u
            _row_copy(h_hbm, srcn_ref[r], xbuf_ref.at[nxt], r, sems.at[nxt]).start()
        swiglu()

    @pl.when(busy & jnp.logical_not(feeds))
    def _():
        swiglu()


def _moe_ffn(hn, src, expert_of, valid, wg, wu, wd):
    T, D = hn.shape
    P = src.shape[0]
    E, _, F = wg.shape
    tm, tf = MOE_TM, 512
    nj = F // tf
    n_tiles = P // tm
    issue_steps = 8
    assert nj >= issue_steps and tm % (issue_steps * ROW_DMA_UNROLL) == 0

    def wcol(i, j, te, va):
        return jnp.where(va[i] == 1, j, nj - 1)

    grid_spec = pltpu.PrefetchScalarGridSpec(
        num_scalar_prefetch=2,
        grid=(n_tiles, nj),
        in_specs=[
            pl.BlockSpec((tm,), lambda i, j, te, va: (0,), memory_space=pltpu.SMEM),
            pl.BlockSpec((tm,), lambda i, j, te, va: (jnp.minimum(i + 1, n_tiles - 1),), memory_space=pltpu.SMEM),
            pl.BlockSpec(memory_space=pl.ANY),
            pl.BlockSpec((None, D, tf), lambda i, j, te, va: (te[i], 0, wcol(i, j, te, va))),
            pl.BlockSpec((None, D, tf), lambda i, j, te, va: (te[i], 0, wcol(i, j, te, va))),
            pl.BlockSpec((None, tf, D), lambda i, j, te, va: (te[i], wcol(i, j, te, va), 0)),
        ],
        out_specs=pl.BlockSpec((tm, D), lambda i, j, te, va: (i, 0)),
        scratch_shapes=[pltpu.VMEM((2, tm, D), F32), pltpu.VMEM((tm, D), BF16), pltpu.SemaphoreType.DMA((2,))],
    )
    return pl.pallas_call(
        functools.partial(_moe_ffn_kernel, n_tiles=n_tiles, issue_steps=issue_steps),
        grid_spec=grid_spec,
        out_shape=jax.ShapeDtypeStruct((P, D), F32),
        compiler_params=_cparams("arbitrary", "arbitrary"),
        name="moe_ffn",
    )(expert_of, valid, src, src, hn, wg, wu, wd)


def _combine_kernel(p1_ref, p2_ref, sel_ref, x_ref, gt_ref, ys_hbm, o_ref, y1_ref, y2_ref, sem1, sem2):
    n = y1_ref.shape[0]
    _start_rows(p1_ref, ys_hbm, y1_ref, sem1, 0, n)
    _start_rows(p2_ref, ys_hbm, y2_ref, sem2, 0, n)
    _wait_rows(ys_hbm, y1_ref, sem1)
    _wait_rows(ys_hbm, y2_ref, sem2)
    sel = sel_ref[...]
    lane = lax.broadcasted_iota(jnp.int32, sel.shape, 1)
    g1 = _lane_pick(sel, lane, SEL_G1)
    g2 = _lane_pick(sel, lane, SEL_G2)
    o_ref[...] = x_ref[...] + gt_ref[...] * (g1 * y1_ref[...] + g2 * y2_ref[...])


def _combine(ys, pos1, pos2, sel, x, mod, S):
    T, D = x.shape
    tm = 512
    idx = pl.BlockSpec((tm,), lambda i: (i,), memory_space=pltpu.SMEM)
    return pl.pallas_call(
        _combine_kernel,
        grid=(T // tm,),
        in_specs=[idx, idx,
                  pl.BlockSpec((tm, LANES), lambda i: (i, 0)),
                  pl.BlockSpec((tm, D), lambda i: (i, 0)),
                  _mod_spec(5, tm, S),
                  pl.BlockSpec(memory_space=pl.ANY)],
        out_specs=pl.BlockSpec((tm, D), lambda i: (i, 0)),
        out_shape=jax.ShapeDtypeStruct((T, D), F32),
        scratch_shapes=[pltpu.VMEM((tm, D), F32), pltpu.VMEM((tm, D), F32),
                        pltpu.SemaphoreType.DMA(()), pltpu.SemaphoreType.DMA(())],
        compiler_params=_cparams("arbitrary"),
        name="moe_combine",
    )(pos1, pos2, sel, x, mod, ys)


def _moe(x, g, mod, wr_pad, wg, wu, wd, S):
    T = x.shape[0]
    n_tiles = (2 * T) // MOE_TM + N_EXPERTS
    hn, sel, cnt = _router(x, g, mod, wr_pad, S)
    pos1, pos2, src, expert_of, valid = _route_tables(sel, cnt, T, n_tiles)
    ys = _moe_ffn(hn, src, expert_of, valid, wg, wu, wd)
    return _combine(ys, pos1, pos2, sel, x, mod, S)


def _cast_kernel(w_ref, o_ref):
    o_ref[...] = w_ref[...].astype(BF16)


def _cast_layer(w, l):
    if w.ndim == 3:
        return _cast_layer(w[:, None], l)[0]
    _, E, R, C = w.shape
    tr = 256
    return pl.pallas_call(
        _cast_kernel,
        grid=(E, R // tr),
        in_specs=[pl.BlockSpec((None, None, tr, C), lambda e, r: (l, e, r, 0))],
        out_specs=pl.BlockSpec((None, tr, C), lambda e, r: (e, r, 0)),
        out_shape=jax.ShapeDtypeStruct((E, R, C), BF16),
        compiler_params=_cparams("parallel", "parallel"),
        name="cast_bf16",
    )(w)


def _final_kernel(x_ref, g_ref, o_ref):
    x = x_ref[...]
    o_ref[...] = x * lax.rsqrt(jnp.mean(x * x, axis=-1, keepdims=True) + NORM_EPS) * g_ref[...]


def _final_norm(x, g):
    T, D = x.shape
    tm = 1024
    return pl.pallas_call(
        _final_kernel,
        grid=(T // tm,),
        in_specs=[pl.BlockSpec((tm, D), lambda i: (i, 0)), pl.BlockSpec((1, D), lambda i: (0, 0))],
        out_specs=pl.BlockSpec((tm, D), lambda i: (i, 0)),
        out_shape=jax.ShapeDtypeStruct((T, D), F32),
        compiler_params=_cparams("parallel"),
        name="final_norm",
    )(x, g.reshape(1, D))


def _prep_layer(l, p):
    t5 = p["t5_table"]
    sw_tab = jnp.stack([t5[:, h] for h in SW_PERM])
    dil_tab = t5[:, SW_HEADS:].reshape(T5_BUCKETS, DIL_GROUPS, DIL_HEADS).transpose(1, 2, 0)
    w_in = p["w_in"][l]
    out = {
        "w_main": _take(w_in, _MAIN_RUNS, 1),
        "w_dil": [_take(w_in, _DIL_RUNS[g], 1) for g in range(DIL_GROUPS)],
        "w_uq": _take(p["mla_w_uq"][l], _UQ_RUNS, 1),
        "w_ukv": _take(p["mla_w_ukv"][l], _UKV_RUNS, 1),
        "w_out": _take(p["w_out"][l], _OUT_RUNS, 0),
        "na_bias": _na_bias(p["na_rpb"][l]),
        "sw_bias": _band_bias(sw_tab, SW_RADIUS, 1, LANES, LANES + 2 * SW_RADIUS),
        "sinks": jnp.stack([p["sw_sinks"][l][h] for h in SW_PERM]) * LOG2E,
        "dil_tab": dil_tab,
    }
    if l % 2 == 0:
        out["wg"] = _cast_layer(p["ffn_w_gate"], l // 2)
        out["wu"] = _cast_layer(p["ffn_w_up"], l // 2)
        out["wd"] = _cast_layer(p["ffn_w_down"], l // 2)
    else:
        out["wr"] = jnp.pad(p["moe_w_router"][l // 2], ((0, 0), (0, LANES - N_EXPERTS)))
        out["wg"] = _cast_layer(p["moe_w_gate"], l // 2)
        out["wu"] = _cast_layer(p["moe_w_up"], l // 2)
        out["wd"] = _cast_layer(p["moe_w_down"], l // 2)
    return out


def _token_mix(proj, hn, lw, p, l, B, S, rope):
    o_a = _na_attn(proj, lw["na_bias"], B, S)
    qm, km, vm = _mla_up(proj, rope[0], rope[1], p["mla_g_q"][l], p["mla_g_kv"][l], lw["w_uq"], lw["w_ukv"], S)
    o_b = _mla_attn(qm, km, vm, B, S)
    o_c = _sw_attn(proj, lw["sw_bias"], lw["sinks"], B, S)
    outs = []
    for g, dil in enumerate(DIL_DILATIONS):
        L = S // dil
        bias = _band_bias(lw["dil_tab"][g], DIL_SIDE, dil, LANES, min(L, LANES + 2 * DIL_SIDE))
        outs.append(_dil_attn(_dil_proj(hn, lw["w_dil"][g], B, S, dil), bias, dil))
    o_d = _merge(outs, B, S)
    return o_a, o_b, o_c, o_d


def _run_trunk(x, mods, layer_w, p, B, S):
    T = B * S
    x = x.reshape(T, D_MODEL)
    rope = _rope_tables(S)
    for l in range(DEPTH):
        lw = layer_w[l]
        mod = mods[l]
        proj, hn = _in_proj(x, p["g_mix"][l], mod, lw["w_main"], S)
        o_a, o_b, o_c, o_d = _token_mix(proj, hn, lw, p, l, B, S, rope)
        x = _out_proj(o_a, o_b, o_c, o_d, lw["w_out"], x, mod, S)
        if l % 2 == 0:
            x = _ffn(x, p["g_ffn"][l], mod, lw["wg"], lw["wu"], lw["wd"], S)
        else:
            x = _moe(x, p["g_ffn"][l], mod, lw["wr"], lw["wg"], lw["wu"], lw["wd"], S)
    return _final_norm(x, p["g_final"]).reshape(B, S, D_MODEL)


def kernel(x_prompt, x_sample, c_prompt, c_sample, w_ada, b_ada, g_mix, g_ffn, w_in, mla_g_q, mla_g_kv, mla_w_uq, mla_w_ukv, na_rpb, sw_sinks, t5_table, w_out, ffn_w_gate, ffn_w_up, ffn_w_down, moe_w_router, moe_w_gate, moe_w_up, moe_w_down, g_final):
    p = dict(g_mix=g_mix, g_ffn=g_ffn, w_in=w_in, mla_g_q=mla_g_q, mla_g_kv=mla_g_kv, mla_w_uq=mla_w_uq,
             mla_w_ukv=mla_w_ukv, na_rpb=na_rpb, sw_sinks=sw_sinks, t5_table=t5_table, w_out=w_out,
             ffn_w_gate=ffn_w_gate, ffn_w_up=ffn_w_up, ffn_w_down=ffn_w_down, moe_w_router=moe_w_router,
             moe_w_gate=moe_w_gate, moe_w_up=moe_w_up, moe_w_down=moe_w_down, g_final=g_final)
    Bp, Sp, _ = x_prompt.shape
    Bs, Ss, _ = x_sample.shape
    rows = 16
    c_pad = jnp.concatenate([c_prompt, c_sample, jnp.zeros((rows - Bp - Bs, D_MODEL), F32)], axis=0)
    mod_all = _ada(c_pad, w_ada, b_ada)
    mods_p = [mod_all[l, :Bp].reshape(Bp, 6, 1, D_MODEL) for l in range(DEPTH)]
    mods_s = [mod_all[l, Bp:Bp + Bs].reshape(Bs, 6, 1, D_MODEL) for l in range(DEPTH)]
    layer_w = [_prep_layer(l, p) for l in range(DEPTH)]
    y_prompt = _run_trunk(x_prompt, mods_p, layer_w, p, Bp, Sp)
    y_sample = _run_trunk(x_sample, mods_s, layer_w, p, Bs, Ss)
    return (y_prompt, y_sample)
```
